```python
import math
import jax, jax.numpy as jnp
from jax import lax
import numpy as np

D_MODEL = 4096
BATCH = 2
SEQ = 8192
DEPTH = 1
DEC_BATCH = 4
DEC_SEQ = 4096
PAST_LEN = 128

HEAD_DIM_A = 128
N_HEADS_A = D_MODEL // (4 * HEAD_DIM_A)
W_A = N_HEADS_A * 2 * HEAD_DIM_A
Q_BLOCK = 128
N_HEADS_B = 4
DV_B = D_MODEL // (2 * N_HEADS_B)
DK_B = DV_B // 2
W_B = N_HEADS_B * DV_B
QK_B = N_HEADS_B * DK_B
CHUNK = 128
N_IN = 3 * W_A + 2 * QK_B + 2 * W_B + 4 * N_HEADS_B + 2 * D_MODEL
N_GROUPS = 4
EXP_PER_GROUP = 8
N_EXP = N_GROUPS * EXP_PER_GROUP
TOP_K = 2
D_FF_EXP = D_MODEL // 4
MOE_BLOCK = 128

EPS = 1e-6
NEG = -1e30

kernel_name = "hybrid_diffattn_mlstm_hmoe_encoder"


def rms_norm(x, g):
    xf = x.astype(jnp.float32)
    y = xf * lax.rsqrt(jnp.mean(xf * xf, axis=-1, keepdims=True) + EPS)
    return (y * g.astype(jnp.float32)).astype(x.dtype)


def alibi_slopes(n):
    return 2.0 ** (-8.0 * (jnp.arange(n, dtype=jnp.float32) + 1.0) / n)


def diff_attention(q, k, v, q_g, k_g, lam, lam_init, subln_g):
    B_, S_ = q.shape[:2]
    nQ = S_ // Q_BLOCK
    q = rms_norm(q, q_g) * (HEAD_DIM_A ** -0.5)
    k = rms_norm(k, k_g)
    kt = k.transpose(0, 2, 3, 1, 4)
    vt = v.transpose(0, 2, 1, 3)
    qb = q.reshape(B_, nQ, Q_BLOCK, N_HEADS_A, 2, HEAD_DIM_A).transpose(1, 0, 3, 4, 2, 5)
    slopes = alibi_slopes(N_HEADS_A)
    kpos = jnp.arange(S_, dtype=jnp.float32)

    def block(args):
        qblk, start = args
        qpos = start + jnp.arange(Q_BLOCK, dtype=jnp.float32)
        bias = -slopes[:, None, None] * jnp.abs(qpos[:, None] - kpos[None, :])
        logits = jnp.einsum('bhcqd,bhckd->bhcqk', qblk, kt).astype(jnp.float32) + bias[None, :, None]
        p = jax.nn.softmax(logits, axis=-1)
        a = p[:, :, 0] - lam * p[:, :, 1]
        return jnp.einsum('bhqk,bhkv->bhqv', a.astype(vt.dtype), vt)

    starts = jnp.arange(nQ, dtype=jnp.float32) * Q_BLOCK
    o = lax.map(block, (qb, starts))
    o = o.transpose(1, 0, 3, 2, 4).reshape(B_, S_, N_HEADS_A, 2 * HEAD_DIM_A)
    o = rms_norm(o, subln_g) * (1.0 - lam_init)
    return o.reshape(B_, S_, W_A)


def mlstm_dir(q, k, v, ig, lf):
    B_, H_, S_, dk = q.shape
    dv = v.shape[-1]
    nC = S_ // CHUNK
    tril = jnp.tril(jnp.ones((CHUNK, CHUNK), dtype=bool))

    def chunks(a):
        return jnp.moveaxis(a.reshape(a.shape[:2] + (nC, CHUNK) + a.shape[3:]), 2, 0)

    def step(carry, inp):
        C, n, m = carry
        qc, kc, vc, ic, lfc = inp
        b = jnp.cumsum(lfc, axis=-1)
        logD = jnp.where(tril, b[..., :, None] - b[..., None, :] + ic[..., None, :], NEG)
        inter = b + m[..., None]
        m_row = jnp.maximum(inter, jnp.max(logD, axis=-1))
        Dm = jnp.exp(logD - m_row[..., None])
        sc = jnp.exp(inter - m_row)
        s = jnp.einsum('bhld,bhsd->bhls', qc, kc) * Dm
        num = jnp.einsum('bhls,bhsv->bhlv', s, vc) + sc[..., None] * jnp.einsum('bhld,bhdv->bhlv', qc, C)
        den = jnp.sum(s, axis=-1) + sc * jnp.einsum('bhld,bhd->bhl', qc, n)
        h = num / jnp.maximum(jnp.abs(den), jnp.exp(-m_row))[..., None]
        bL = b[..., -1]
        w_s = bL[..., None] - b + ic
        m_new = jnp.maximum(bL + m, jnp.max(w_s, axis=-1))
        decay = jnp.exp(bL + m - m_new)
        ws = jnp.exp(w_s - m_new[..., None])
        C_new = decay[..., None, None] * C + jnp.einsum('bhs,bhsd,bhsv->bhdv', ws, kc, vc)
        n_new = decay[..., None] * n + jnp.einsum('bhs,bhsd->bhd', ws, kc)
        return (C_new, n_new, m_new), h

    init = (jnp.zeros((B_, H_, dk, dv), jnp.float32), jnp.zeros((B_, H_, dk), jnp.float32),
            jnp.full((B_, H_), NEG, jnp.float32))
    _, h = lax.scan(step, init, (chunks(q), chunks(k), chunks(v), chunks(ig), chunks(lf)))
    return jnp.moveaxis(h, 0, 2).reshape(B_, H_, S_, dv)


def mlstm_branch(q, k, v, o, g_if, b_if, norm_g):
    B_, S_ = q.shape[:2]
    f32 = jnp.float32
    qb = q.reshape(B_, S_, N_HEADS_B, DK_B).transpose(0, 2, 1, 3).astype(f32) * (DK_B ** -0.5)
    kb = k.reshape(B_, S_, N_HEADS_B, DK_B).transpose(0, 2, 1, 3).astype(f32)
    vb = v.reshape(B_, S_, N_HEADS_B, DV_B).transpose(0, 2, 1, 3).astype(f32)
    gates = (g_if.astype(f32) + b_if.astype(f32)).reshape(B_, S_, 4, N_HEADS_B).transpose(2, 0, 3, 1)
    i_f, f_f, i_b, f_b = gates[0], gates[1], gates[2], gates[3]
    h_f = mlstm_dir(qb, kb, vb, i_f, jax.nn.log_sigmoid(f_f))
    fl = lambda a: jnp.flip(a, axis=2)
    h_b = fl(mlstm_dir(fl(qb), fl(kb), fl(vb), fl(i_b), fl(jax.nn.log_sigmoid(f_b))))
    h = (h_f + h_b).transpose(0, 2, 1, 3)
    h = rms_norm(h, norm_g).reshape(B_, S_, W_B)
    return (jax.nn.sigmoid(o.astype(f32)) * h).astype(q.dtype)


def hier_moe(x, w_group, b_group, w_fine, b_fine, w1, w3, w2):
    B_, S_, D_ = x.shape
    T = B_ * S_
    xt = x.reshape(T, D_)
    coarse = (xt @ w_group).astype(jnp.float32) + b_group.astype(jnp.float32)
    pg = jax.nn.softmax(coarse, axis=-1)
    g = jnp.argmax(coarse, axis=-1)
    p_top = jnp.take_along_axis(pg, g[:, None], axis=1)[:, 0]
    fine = (xt @ w_fine).astype(jnp.float32).reshape(T, N_GROUPS, EXP_PER_GROUP) + b_fine.astype(jnp.float32)
    fine_g = jnp.take_along_axis(fine, g[:, None, None], axis=1)[:, 0]
    pe = jax.nn.softmax(fine_g, axis=-1)
    top_w, top_i = lax.top_k(pe, TOP_K)
    top_w = top_w / jnp.sum(top_w, axis=-1, keepdims=True)
    gate = p_top[:, None] * top_w
    eid = (g[:, None] * EXP_PER_GROUP + top_i).astype(jnp.int32)

    TK = T * TOP_K
    e_flat = eid.reshape(-1)
    w_flat = gate.reshape(-1)
    tok_flat = jnp.repeat(jnp.arange(T, dtype=jnp.int32), TOP_K)
    order = jnp.argsort(e_flat)
    se = e_flat[order]
    counts = jnp.bincount(e_flat, length=N_EXP).astype(jnp.int32)
    pcounts = (counts + MOE_BLOCK - 1) // MOE_BLOCK * MOE_BLOCK
    pends = jnp.cumsum(pcounts)
    pstarts = pends - pcounts
    ustarts = jnp.cumsum(counts) - counts
    dest = pstarts[se] + jnp.arange(TK, dtype=jnp.int32) - ustarts[se]
    NB = -(-TK // MOE_BLOCK) + N_EXP
    slot_tok = jnp.full((NB * MOE_BLOCK,), T, jnp.int32).at[dest].set(tok_flat[order])
    slot_w = jnp.zeros((NB * MOE_BLOCK,), jnp.float32).at[dest].set(w_flat[order])
    blk_exp = jnp.minimum(jnp.searchsorted(pends, jnp.arange(NB, dtype=jnp.int32) * MOE_BLOCK, side='right'),
                          N_EXP - 1).astype(jnp.int32)
    x_pad = jnp.concatenate([xt, jnp.zeros((1, D_), xt.dtype)], axis=0)

    def body(y, inp):
        toks, wts, e = inp
        xb = x_pad[toks]
        h = jax.nn.silu(xb @ w1[e]) * (xb @ w3[e])
        out = (h @ w2[e]) * wts[:, None].astype(xt.dtype)
        return y.at[toks].add(out), None

    y0 = jnp.zeros((T + 1, D_), xt.dtype)
    y, _ = lax.scan(body, y0, (slot_tok.reshape(NB, MOE_BLOCK), slot_w.reshape(NB, MOE_BLOCK), blk_exp))
    return y[:T].reshape(B_, S_, D_)


def encoder_layer(x, l, norm_mix_g, w_in, b_mgate, attn_q_norm_g, attn_k_norm_g, lambda_q1, lambda_k1,
                  lambda_q2, lambda_k2, attn_subln_g, mlstm_norm_g, w_branch_a, w_branch_b, w_out,
                  norm_ffn_g, w_router_group, b_router_group, w_router_expert, b_router_expert,
                  w_exp_gate, w_exp_up, w_exp_down):
    B_, S_, _ = x.shape
    xn = rms_norm(x, norm_mix_g[l])
    proj = xn @ w_in[l]
    sizes = [W_A, W_A, W_A, QK_B, QK_B, W_B, W_B, 4 * N_HEADS_B, D_MODEL, D_MODEL]
    offs = np.cumsum(sizes)[:-1].tolist()
    qa, ka, va, qm, km, vm, om, gif, gate_a, gate_b = jnp.split(proj, offs, axis=-1)
    lam_init = 0.8 - 0.6 * math.exp(-0.3 * l)
    f32 = jnp.float32
    lam = (jnp.exp(jnp.sum(lambda_q1[l].astype(f32) * lambda_k1[l].astype(f32)))
           - jnp.exp(jnp.sum(lambda_q2[l].astype(f32) * lambda_k2[l].astype(f32))) + lam_init)
    ya = diff_attention(qa.reshape(B_, S_, N_HEADS_A, 2, HEAD_DIM_A),
                        ka.reshape(B_, S_, N_HEADS_A, 2, HEAD_DIM_A),
                        va.reshape(B_, S_, N_HEADS_A, 2 * HEAD_DIM_A),
                        attn_q_norm_g[l], attn_k_norm_g[l], lam, lam_init, attn_subln_g[l])
    yb = mlstm_branch(qm, km, vm, om, gif, b_mgate[l], mlstm_norm_g[l])
    merged = jax.nn.sigmoid(gate_a) * (ya @ w_branch_a[l]) + jax.nn.sigmoid(gate_b) * (yb @ w_branch_b[l])
    x = x + merged @ w_out[l]
    x = x + hier_moe(rms_norm(x, norm_ffn_g[l]), w_router_group[l], b_router_group[l], w_router_expert[l],
                     b_router_expert[l], w_exp_gate[l], w_exp_up[l], w_exp_down[l])
    return x


def setup_inputs(seed: int = 0) -> dict:
    key = jax.random.key(seed)
    ks = jax.random.split(key, 32)
    nrm = lambda k, shape, s: jax.random.normal(k, shape, jnp.float32) * s
    gain = lambda k, shape: 1.0 + 0.05 * jax.random.normal(k, shape, jnp.float32)
    L = DEPTH
    gate_base = jnp.repeat(jnp.array([0.0, 3.0, 0.0, 3.0], jnp.float32), N_HEADS_B)
    return {
        "x_prompt": nrm(ks[0], (BATCH, SEQ, D_MODEL), 1.0),
        "x_sample": nrm(ks[1], (DEC_BATCH, DEC_SEQ, D_MODEL), 1.0),
        "norm_mix_g": gain(ks[2], (L, D_MODEL)),
        "w_in": nrm(ks[3], (L, D_MODEL, N_IN), D_MODEL ** -0.5),
        "b_mgate": gate_base + nrm(ks[4], (L, 4 * N_HEADS_B), 0.1),
        "attn_q_norm_g": gain(ks[5], (L, 2, HEAD_DIM_A)),
        "attn_k_norm_g": gain(ks[6], (L, 2, HEAD_DIM_A)),
        "lambda_q1": nrm(ks[7], (L, HEAD_DIM_A), 0.1),
        "lambda_k1": nrm(ks[8], (L, HEAD_DIM_A), 0.1),
        "lambda_q2": nrm(ks[9], (L, HEAD_DIM_A), 0.1),
        "lambda_k2": nrm(ks[10], (L, HEAD_DIM_A), 0.1),
        "attn_subln_g": gain(ks[11], (L, 2 * HEAD_DIM_A)),
        "mlstm_norm_g": gain(ks[12], (L, DV_B)),
        "w_branch_a": nrm(ks[13], (L, W_A, D_MODEL), W_A ** -0.5),
        "w_branch_b": nrm(ks[14], (L, W_B, D_MODEL), W_B ** -0.5),
        "w_out": nrm(ks[15], (L, D_MODEL, D_MODEL), D_MODEL ** -0.5),
        "norm_ffn_g": gain(ks[16], (L, D_MODEL)),
        "w_router_group": nrm(ks[17], (L, D_MODEL, N_GROUPS), D_MODEL ** -0.5),
        "b_router_group": nrm(ks[18], (L, N_GROUPS), 0.01),
        "w_router_expert": nrm(ks[19], (L, D_MODEL, N_GROUPS * EXP_PER_GROUP), D_MODEL ** -0.5),
        "b_router_expert": nrm(ks[20], (L, N_GROUPS, EXP_PER_GROUP), 0.01),
        "w_exp_gate": nrm(ks[21], (L, N_EXP, D_MODEL, D_FF_EXP), D_MODEL ** -0.5),
        "w_exp_up": nrm(ks[22], (L, N_EXP, D_MODEL, D_FF_EXP), D_MODEL ** -0.5),
        "w_exp_down": nrm(ks[23], (L, N_EXP, D_FF_EXP, D_MODEL), D_FF_EXP ** -0.5),
    }


def reference(x_prompt, x_sample, norm_mix_g, w_in, b_mgate, attn_q_norm_g, attn_k_norm_g, lambda_q1,
              lambda_k1, lambda_q2, lambda_k2, attn_subln_g, mlstm_norm_g, w_branch_a, w_branch_b, w_out,
              norm_ffn_g, w_router_group, b_router_group, w_router_expert, b_router_expert,
              w_exp_gate, w_exp_up, w_exp_down):
    def trunk(x):
        for l in range(DEPTH):
            x = encoder_layer(x, l, norm_mix_g, w_in, b_mgate, attn_q_norm_g, attn_k_norm_g, lambda_q1,
                              lambda_k1, lambda_q2, lambda_k2, attn_subln_g, mlstm_norm_g, w_branch_a,
                              w_branch_b, w_out, norm_ffn_g, w_router_group, b_router_group,
                              w_router_expert, b_router_expert, w_exp_gate, w_exp_up, w_exp_down)
        return x

    y_prompt = trunk(x_prompt)
    y_sample = trunk(x_sample)
    return (y_prompt, y_sample)
```

```python
import functools

import jax
import jax.numpy as jnp
from jax import lax
from jax.experimental import pallas as pl
from jax.experimental.pallas import tpu as pltpu

D_MODEL = 4096
HEAD_DIM_A = 128
N_HEADS_A = 8
W_A = 2048
N_HEADS_B = 4
DV_B = 512
DK_B = 256
W_B = 2048
QK_B = 1024
CHUNK = 128
N_GROUPS = 4
EXP_PER_GROUP = 8
N_EXP = 32
D_FF_EXP = 1024
EPS = 1e-6
NEG = -1e30
LAM_INIT = 0.8 - 0.6 * 1.0

LANES = 128
MOE_ROWS = 256
FF_CHUNK = 512
VMEM_LIMIT = 56 * 1024 * 1024

F32 = jnp.float32
BF16 = jnp.bfloat16
HIGHEST = lax.Precision.HIGHEST


def _params(sem, vmem=VMEM_LIMIT):
    return pltpu.CompilerParams(dimension_semantics=sem, vmem_limit_bytes=vmem)


def _rmsnorm_kernel(x_ref, g_ref, o_ref):
    x = x_ref[...]
    ms = jnp.mean(x * x, axis=-1, keepdims=True)
    o_ref[...] = (x * lax.rsqrt(ms + EPS) * g_ref[...]).astype(o_ref.dtype)


def _rmsnorm(x, g, tm):
    T, D = x.shape
    return pl.pallas_call(
        _rmsnorm_kernel,
        grid=(T // tm,),
        in_specs=[pl.BlockSpec((tm, D), lambda i: (i, 0)), pl.BlockSpec((1, D), lambda i: (0, 0))],
        out_specs=pl.BlockSpec((tm, D), lambda i: (i, 0)),
        out_shape=jax.ShapeDtypeStruct((T, D), BF16),
        compiler_params=_params(("parallel",)),
        name="rmsnorm",
    )(x, g.reshape(1, D))


def _proj_kernel(a_ref, w_ref, o_ref):
    o_ref[...] = jnp.dot(a_ref[...], w_ref[...], preferred_element_type=F32).astype(o_ref.dtype)


def _proj_bias_kernel(a_ref, w_ref, b_ref, o_ref):
    acc = jnp.dot(a_ref[...], w_ref[...], preferred_element_type=F32)
    o_ref[...] = (acc + b_ref[...]).astype(o_ref.dtype)


def _proj_headnorm_kernel(a_ref, w_ref, g_ref, o_ref):
    acc = jnp.dot(a_ref[...], w_ref[...], preferred_element_type=F32)
    for j in range(acc.shape[1] // HEAD_DIM_A):
        sl = slice(j * HEAD_DIM_A, (j + 1) * HEAD_DIM_A)
        blk = acc[:, sl]
        ms = jnp.mean(blk * blk, axis=-1, keepdims=True)
        o_ref[:, sl] = (blk * lax.rsqrt(ms + EPS) * g_ref[:, sl]).astype(o_ref.dtype)


def _proj(a, w, n_out, out_dtype, tm, tn, *, col_off=0, vec=None, kind="plain", name="proj"):
    T, K = a.shape
    assert n_out % tn == 0 and col_off % tn == 0 and T % tm == 0
    off = col_off // tn
    in_specs = [pl.BlockSpec((tm, K), lambda i, j: (i, 0)),
                pl.BlockSpec((K, tn), lambda i, j: (0, j + off))]
    args = [a, w]
    if kind == "plain":
        body = _proj_kernel
    else:
        body = _proj_bias_kernel if kind == "bias" else _proj_headnorm_kernel
        in_specs.append(pl.BlockSpec((1, tn), lambda i, j: (0, j)))
        args.append(vec.reshape(1, n_out).astype(F32))
    return pl.pallas_call(
        body,
        grid=(T // tm, n_out // tn),
        in_specs=in_specs,
        out_specs=pl.BlockSpec((tm, tn), lambda i, j: (i, j)),
        out_shape=jax.ShapeDtypeStruct((T, n_out), out_dtype),
        compiler_params=_params(("parallel", "arbitrary")),
        name=name,
    )(*args)


def _attn_kernel(slopes_ref, lam_ref, q_ref, k_ref, v_ref, g_ref, o_ref, m_sc, l_sc, acc_sc, *, tq, tk):
    h = pl.program_id(1)
    qi = pl.program_id(2)
    ki = pl.program_id(3)

    @pl.when(ki == 0)
    def _():
        m_sc[...] = jnp.full(m_sc.shape, NEG, F32)
        l_sc[...] = jnp.zeros(l_sc.shape, F32)
        acc_sc[...] = jnp.zeros(acc_sc.shape, F32)

    row = lax.broadcasted_iota(jnp.int32, (tq, tk), 0)
    col = lax.broadcasted_iota(jnp.int32, (tq, tk), 1)
    dist = jnp.abs((row - col + (qi * tq - ki * tk)).astype(F32))
    bias = -slopes_ref[h] * dist
    v = v_ref[...]
    for c in range(2):
        sl = slice(c * HEAD_DIM_A, (c + 1) * HEAD_DIM_A)
        s = lax.dot_general(q_ref[:, sl], k_ref[:, sl], (((1,), (1,)), ((), ())),
                            preferred_element_type=F32) + bias
        m_prev = m_sc[c]
        m_new = jnp.maximum(m_prev, jnp.max(s, axis=-1, keepdims=True))
        alpha = jnp.exp(m_prev - m_new)
        p = jnp.exp(s - m_new)
        l_sc[c] = alpha * l_sc[c] + jnp.sum(p, axis=-1, keepdims=True)
        acc_sc[c] = alpha * acc_sc[c] + jnp.dot(p.astype(BF16), v, preferred_element_type=F32)
        m_sc[c] = m_new

    @pl.when(ki == pl.num_programs(3) - 1)
    def _():
        o = acc_sc[0] / l_sc[0] - lam_ref[0] * (acc_sc[1] / l_sc[1])
        ms = jnp.mean(o * o, axis=-1, keepdims=True)
        o_ref[...] = (o * lax.rsqrt(ms + EPS) * g_ref[...]).astype(o_ref.dtype)


def _diff_attention(qk, vqkv, slopes, lam, subln_scaled, B, S, tq, tk):
    T = B * S
    nq, nk = S // tq, S // tk
    hw = 2 * HEAD_DIM_A
    return pl.pallas_call(
        functools.partial(_attn_kernel, tq=tq, tk=tk),
        grid=(B, N_HEADS_A, nq, nk),
        in_specs=[
            pl.BlockSpec(memory_space=pltpu.SMEM),
            pl.BlockSpec(memory_space=pltpu.SMEM),
            pl.BlockSpec((tq, hw), lambda b, h, i, j: (b * nq + i, h)),
            pl.BlockSpec((tk, hw), lambda b, h, i, j: (b * nk + j, N_HEADS_A + h)),
            pl.BlockSpec((tk, hw), lambda b, h, i, j: (b * nk + j, h)),
            pl.BlockSpec((1, hw), lambda b, h, i, j: (0, 0)),
        ],
        out_specs=pl.BlockSpec((tq, hw), lambda b, h, i, j: (b * nq + i, h)),
        out_shape=jax.ShapeDtypeStruct((T, W_A), BF16),
        scratch_shapes=[pltpu.VMEM((2, tq, 1), F32), pltpu.VMEM((2, tq, 1), F32),
                        pltpu.VMEM((2, tq, hw), F32)],
        compiler_params=_params(("parallel", "parallel", "parallel", "arbitrary")),
        name="diff_attention",
    )(slopes, lam, qk, qk, vqkv, subln_scaled.reshape(1, hw))


def _log_sigmoid(x):
    return jnp.minimum(x, 0.0) - jnp.log1p(jnp.exp(-jnp.abs(x)))


def _mlstm_chunk(q, k, v, gcol, grow, c_ref, n_ref, m_ref, reverse):
    L = q.shape[0]
    ii, fi = (2, 3) if reverse else (0, 1)
    r = lax.broadcasted_iota(jnp.int32, (L, L), 0)
    c = lax.broadcasted_iota(jnp.int32, (L, L), 1)
    lower = c <= r
    upper = c >= r
    tri_lo = jnp.where(lower, 1.0, 0.0).astype(F32)
    tri_up = jnp.where(upper, 1.0, 0.0).astype(F32)
    ic_row = grow[ii:ii + 1, :]
    ic_col = gcol[:, ii:ii + 1]
    lf_row = _log_sigmoid(grow[fi:fi + 1, :])
    lf_col = _log_sigmoid(gcol[:, fi:fi + 1])
    if reverse:
        b_row = jnp.dot(lf_row, tri_lo, precision=HIGHEST, preferred_element_type=F32)
        b_col = jnp.dot(tri_up, lf_col, precision=HIGHEST, preferred_element_type=F32)
        mask = upper
        b_last = b_row[:, 0:1]
    else:
        b_row = jnp.dot(lf_row, tri_up, precision=HIGHEST, preferred_element_type=F32)
        b_col = jnp.dot(tri_lo, lf_col, precision=HIGHEST, preferred_element_type=F32)
        mask = lower
        b_last = b_row[:, L - 1:L]
    m = m_ref[...]
    log_d = jnp.where(mask, b_col - b_row + ic_row, NEG)
    inter = b_col + m
    m_row = jnp.maximum(inter, jnp.max(log_d, axis=-1, keepdims=True))
    d_m = jnp.exp(log_d - m_row)
    sc = jnp.exp(inter - m_row)
    qs = q * (DK_B ** -0.5)
    s = lax.dot_general(qs, k, (((1,), (1,)), ((), ())), preferred_element_type=F32) * d_m
    c_state = c_ref[...]
    n_state = n_ref[...]
    num = (jnp.dot(s.astype(BF16), v, preferred_element_type=F32)
           + sc * jnp.dot(qs, c_state.astype(BF16), preferred_element_type=F32))
    den = (jnp.sum(s, axis=-1, keepdims=True)
           + sc * jnp.sum(qs.astype(F32) * n_state, axis=-1, keepdims=True))
    h = num / jnp.maximum(jnp.abs(den), jnp.exp(-m_row))
    w_col = b_last - b_col + ic_col
    w_row = b_last - b_row + ic_row
    m_new = jnp.maximum(b_last + m, jnp.max(w_row, axis=-1, keepdims=True))
    decay = jnp.exp(b_last + m - m_new)
    ws_col = jnp.exp(w_col - m_new)
    wv = (ws_col * v.astype(F32)).astype(BF16)
    c_ref[...] = decay * c_state + lax.dot_general(k, wv, (((0,), (0,)), ((), ())),
                                                   preferred_element_type=F32)
    n_ref[...] = decay * n_state + jnp.sum(ws_col * k.astype(F32), axis=0, keepdims=True)
    m_ref[...] = m_new
    return h


def _mlstm_kernel(qf_ref, kf_ref, vf_ref, gcf_ref, grf_ref, qb_ref, kb_ref, vb_ref, gcb_ref, grb_ref,
                  hf_ref, hb_ref, c_sc, n_sc, m_sc):
    @pl.when(pl.program_id(2) == 0)
    def _():
        c_sc[...] = jnp.zeros(c_sc.shape, F32)
        n_sc[...] = jnp.zeros(n_sc.shape, F32)
        m_sc[...] = jnp.full(m_sc.shape, NEG, F32)

    hf_ref[...] = _mlstm_chunk(qf_ref[...], kf_ref[...], vf_ref[...], gcf_ref[...], grf_ref[...],
                               c_sc.at[0], n_sc.at[0], m_sc.at[0], reverse=False)
    hb_ref[...] = _mlstm_chunk(qb_ref[...], kb_ref[...], vb_ref[...], gcb_ref[...], grb_ref[...],
                               c_sc.at[1], n_sc.at[1], m_sc.at[1], reverse=True)


def _mlstm(vqkv, gcol, grow, B, S):
    T = B * S
    nc = S // CHUNK
    q_off = W_A // DK_B
    k_off = (W_A + QK_B) // DK_B
    v_off = (W_A + 2 * QK_B) // DV_B

    def fwd(b, h, c):
        return b * nc + c

    def bwd(b, h, c):
        return b * nc + nc - 1 - c

    def specs(pos):
        return [
            pl.BlockSpec((CHUNK, DK_B), lambda b, h, c: (pos(b, h, c), q_off + h)),
            pl.BlockSpec((CHUNK, DK_B), lambda b, h, c: (pos(b, h, c), k_off + h)),
            pl.BlockSpec((CHUNK, DV_B), lambda b, h, c: (pos(b, h, c), v_off + h)),
            pl.BlockSpec((None, None, CHUNK, 4), lambda b, h, c: (b, h, pos(b, h, c) - b * nc, 0)),
            pl.BlockSpec((None, None, 4, CHUNK), lambda b, h, c: (b, h, 0, pos(b, h, c) - b * nc)),
        ]

    out_sds = jax.ShapeDtypeStruct((T, W_B), F32)
    return pl.pallas_call(
        _mlstm_kernel,
        grid=(B, N_HEADS_B, nc),
        in_specs=specs(fwd) + specs(bwd),
        out_specs=[pl.BlockSpec((CHUNK, DV_B), lambda b, h, c: (fwd(b, h, c), h)),
                   pl.BlockSpec((CHUNK, DV_B), lambda b, h, c: (bwd(b, h, c), h))],
        out_shape=[out_sds, out_sds],
        scratch_shapes=[pltpu.VMEM((2, DK_B, DV_B), F32), pltpu.VMEM((2, 1, DK_B), F32),
                        pltpu.VMEM((2, 1, 1), F32)],
        compiler_params=_params(("parallel", "parallel", "arbitrary")),
        name="mlstm",
    )(vqkv, vqkv, vqkv, gcol, grow, vqkv, vqkv, vqkv, gcol, grow)


def _mlstm_out_kernel(hf_ref, hb_ref, o_ref, g_ref, y_ref):
    for j in range(N_HEADS_B):
        sl = slice(j * DV_B, (j + 1) * DV_B)
        h = hf_ref[:, sl] + hb_ref[:, sl]
        ms = jnp.mean(h * h, axis=-1, keepdims=True)
        hn = h * lax.rsqrt(ms + EPS) * g_ref[...]
        y_ref[:, sl] = (jax.nn.sigmoid(o_ref[:, sl]) * hn).astype(y_ref.dtype)


def _mlstm_out(hf, hb, om, norm_g, tm):
    T = hf.shape[0]
    row = pl.BlockSpec((tm, W_B), lambda i: (i, 0))
    return pl.pallas_call(
        _mlstm_out_kernel,
        grid=(T // tm,),
        in_specs=[row, row, row, pl.BlockSpec((1, DV_B), lambda i: (0, 0))],
        out_specs=row,
        out_shape=jax.ShapeDtypeStruct((T, W_B), BF16),
        compiler_params=_params(("parallel",)),
        name="mlstm_out",
    )(hf, hb, om, norm_g.reshape(1, DV_B))


def _merge_kernel(ya_ref, yb_ref, wa_ref, wb_ref, ga_ref, gb_ref, o_ref):
    pa = jnp.dot(ya_ref[...], wa_ref[...], preferred_element_type=F32)
    pb = jnp.dot(yb_ref[...], wb_ref[...], preferred_element_type=F32)
    o_ref[...] = (jax.nn.sigmoid(ga_ref[...]) * pa + jax.nn.sigmoid(gb_ref[...]) * pb).astype(o_ref.dtype)


def _merge(ya, yb, wa, wb, gates, tm, tn):
    T = ya.shape[0]
    nb = D_MODEL // tn
    return pl.pallas_call(
        _merge_kernel,
        grid=(T // tm, nb),
        in_specs=[pl.BlockSpec((tm, W_A), lambda i, j: (i, 0)),
                  pl.BlockSpec((tm, W_B), lambda i, j: (i, 0)),
                  pl.BlockSpec((W_A, tn), lambda i, j: (0, j)),
                  pl.BlockSpec((W_B, tn), lambda i, j: (0, j)),
                  pl.BlockSpec((tm, tn), lambda i, j: (i, j)),
                  pl.BlockSpec((tm, tn), lambda i, j: (i, nb + j))],
        out_specs=pl.BlockSpec((tm, tn), lambda i, j: (i, j)),
        out_shape=jax.ShapeDtypeStruct((T, D_MODEL), BF16),
        compiler_params=_params(("parallel", "arbitrary")),
        name="merge",
    )(ya, yb, wa, wb, gates, gates)


def _outproj_kernel(a_ref, w_ref, x_ref, o_ref):
    o_ref[...] = x_ref[...] + jnp.dot(a_ref[...], w_ref[...], preferred_element_type=F32)


def _outproj(merged, w, x, tm, tn):
    T = merged.shape[0]
    return pl.pallas_call(
        _outproj_kernel,
        grid=(T // tm, D_MODEL // tn),
        in_specs=[pl.BlockSpec((tm, D_MODEL), lambda i, j: (i, 0)),
                  pl.BlockSpec((D_MODEL, tn), lambda i, j: (0, j)),
                  pl.BlockSpec((tm, tn), lambda i, j: (i, j))],
        out_specs=pl.BlockSpec((tm, tn), lambda i, j: (i, j)),
        out_shape=jax.ShapeDtypeStruct((T, D_MODEL), F32),
        compiler_params=_params(("parallel", "arbitrary")),
        name="outproj",
    )(merged, w, x)


def _router_kernel(x_ref, g_ref, wr_ref, br_ref, xn_ref, r_ref):
    x = x_ref[...]
    ms = jnp.mean(x * x, axis=-1, keepdims=True)
    xn = x * lax.rsqrt(ms + EPS) * g_ref[...]
    xn_ref[...] = xn
    logits = jnp.dot(xn, wr_ref[...], precision=HIGHEST, preferred_element_type=F32) + br_ref[...]
    lane = lax.broadcasted_iota(jnp.int32, logits.shape, 1)
    cm = jnp.where(lane < N_GROUPS, logits, NEG)
    cmax = jnp.max(cm, axis=-1, keepdims=True)
    p_top = 1.0 / jnp.sum(jnp.exp(cm - cmax), axis=-1, keepdims=True)
    grp = jnp.min(jnp.where(cm == cmax, lane, LANES), axis=-1, keepdims=True)
    lo = N_GROUPS + EXP_PER_GROUP * grp
    fm = jnp.where((lane >= lo) & (lane < lo + EXP_PER_GROUP), logits, NEG)
    fmax = jnp.max(fm, axis=-1, keepdims=True)
    fe = jnp.exp(fm - fmax)
    pe = fe / jnp.sum(fe, axis=-1, keepdims=True)
    i1 = jnp.min(jnp.where(fm == fmax, lane, LANES), axis=-1, keepdims=True)
    p1 = jnp.sum(jnp.where(lane == i1, pe, 0.0), axis=-1, keepdims=True)
    fm2 = jnp.where(lane == i1, NEG, fm)
    f2max = jnp.max(fm2, axis=-1, keepdims=True)
    i2 = jnp.min(jnp.where(fm2 == f2max, lane, LANES), axis=-1, keepdims=True)
    p2 = jnp.sum(jnp.where(lane == i2, pe, 0.0), axis=-1, keepdims=True)
    g1 = p_top * (p1 / (p1 + p2))
    g2 = p_top * (p2 / (p1 + p2))
    e1 = (i1 - N_GROUPS).astype(F32)
    e2 = (i2 - N_GROUPS).astype(F32)
    r_ref[...] = jnp.where(lane == 0, e1, jnp.where(lane == 1, e2, jnp.where(lane == 2, g1,
                           jnp.where(lane == 3, g2, 0.0))))


def _router(x1, g, wr, br, tm):
    T = x1.shape[0]
    return pl.pallas_call(
        _router_kernel,
        grid=(T // tm,),
        in_specs=[pl.BlockSpec((tm, D_MODEL), lambda i: (i, 0)),
                  pl.BlockSpec((1, D_MODEL), lambda i: (0, 0)),
                  pl.BlockSpec((D_MODEL, LANES), lambda i: (0, 0)),
                  pl.BlockSpec((1, LANES), lambda i: (0, 0))],
        out_specs=[pl.BlockSpec((tm, D_MODEL), lambda i: (i, 0)),
                   pl.BlockSpec((tm, LANES), lambda i: (i, 0))],
        out_shape=[jax.ShapeDtypeStruct((T, D_MODEL), F32), jax.ShapeDtypeStruct((T, LANES), F32)],
        compiler_params=_params(("parallel",)),
        name="router",
    )(x1, g.reshape(1, D_MODEL), wr, br)


def _row_copy(src_hbm, row, dst, r, sem):
    return pltpu.make_async_copy(src_hbm.at[pl.ds(row, 1)], dst.at[pl.ds(r, 1)], sem)


def _gather_kernel(idx_ref, src_hbm, o_ref, sem, *, rows):
    base = pl.program_id(0) * rows

    def start(r, carry):
        _row_copy(src_hbm, idx_ref[base + r], o_ref, r, sem).start()
        return carry

    def wait(r, carry):
        _row_copy(src_hbm, 0, o_ref, r, sem).wait()
        return carry

    lax.fori_loop(0, rows, start, 0)
    lax.fori_loop(0, rows, wait, 0)


def _gather_rows(src, idx, rows):
    n = idx.shape[0]
    D = src.shape[1]
    return pl.pallas_call(
        functools.partial(_gather_kernel, rows=rows),
        grid_spec=pltpu.PrefetchScalarGridSpec(
            num_scalar_prefetch=1,
            grid=(n // rows,),
            in_specs=[pl.BlockSpec(memory_space=pl.ANY)],
            out_specs=pl.BlockSpec((rows, D), lambda i, idx: (i, 0)),
            scratch_shapes=[pltpu.SemaphoreType.DMA(())],
        ),
        out_shape=jax.ShapeDtypeStruct((n, D), src.dtype),
        compiler_params=_params(("arbitrary",)),
        name="moe_gather",
    )(idx, src)


def _moe_kernel(be_ref, nused_ref, xs_ref, w1_ref, w3_ref, w2_ref, sw_ref, o_ref):
    b = pl.program_id(0)
    f = pl.program_id(1)
    last = pl.num_programs(1) - 1

    @pl.when(b < nused_ref[0])
    def _():
        x = xs_ref[...].astype(BF16)
        h1 = jnp.dot(x, w1_ref[...], preferred_element_type=F32)
        h3 = jnp.dot(x, w3_ref[...], preferred_element_type=F32)
        hid = (h1 * jax.nn.sigmoid(h1)) * h3
        out = jnp.dot(hid.astype(BF16), w2_ref[...], preferred_element_type=F32)

        @pl.when(f == 0)
        def _():
            o_ref[...] = out

        @pl.when(f > 0)
        def _():
            o_ref[...] += out

        @pl.when(f == last)
        def _():
            o_ref[...] = o_ref[...] * sw_ref[...]

    @pl.when((b >= nused_ref[0]) & (f == 0))
    def _():
        o_ref[...] = jnp.zeros(o_ref.shape, o_ref.dtype)


def _moe_mlp(xs, w1, w3, w2, slot_w, blk_exp, nused):
    ns = xs.shape[0]
    nb = ns // MOE_ROWS
    nf = D_FF_EXP // FF_CHUNK
    return pl.pallas_call(
        _moe_kernel,
        grid_spec=pltpu.PrefetchScalarGridSpec(
            num_scalar_prefetch=2,
            grid=(nb, nf),
            in_specs=[
                pl.BlockSpec((MOE_ROWS, D_MODEL), lambda b, f, be, nu: (b, 0)),
                pl.BlockSpec((None, D_MODEL, FF_CHUNK), lambda b, f, be, nu: (be[b], 0, f)),
                pl.BlockSpec((None, D_MODEL, FF_CHUNK), lambda b, f, be, nu: (be[b], 0, f)),
                pl.BlockSpec((None, FF_CHUNK, D_MODEL), lambda b, f, be, nu: (be[b], f, 0)),
                pl.BlockSpec((MOE_ROWS, 1), lambda b, f, be, nu: (b, 0)),
            ],
            out_specs=pl.BlockSpec((MOE_ROWS, D_MODEL), lambda b, f, be, nu: (b, 0)),
        ),
        out_shape=jax.ShapeDtypeStruct((ns, D_MODEL), F32),
        compiler_params=_params(("arbitrary", "arbitrary")),
        name="moe_mlp",
    )(blk_exp, nused, xs, w1, w3, w2, slot_w.reshape(ns, 1))


def _combine_kernel(pos_ref, x_ref, o_hbm, y_ref, buf, sem, *, rows):
    base = pl.program_id(0) * rows

    def start(r, carry):
        for k in range(2):
            _row_copy(o_hbm, pos_ref[2 * (base + r) + k], buf.at[k], r, sem).start()
        return carry

    def wait(r, carry):
        for k in range(2):
            _row_copy(o_hbm, 0, buf.at[k], r, sem).wait()
        return carry

    lax.fori_loop(0, rows, start, 0)
    lax.fori_loop(0, rows, wait, 0)
    y_ref[...] = x_ref[...] + (buf[0] + buf[1])


def _combine(x1, o, pos, rows):
    T, D = x1.shape
    return pl.pallas_call(
        functools.partial(_combine_kernel, rows=rows),
        grid_spec=pltpu.PrefetchScalarGridSpec(
            num_scalar_prefetch=1,
            grid=(T // rows,),
            in_specs=[pl.BlockSpec((rows, D), lambda i, pos: (i, 0)),
                      pl.BlockSpec(memory_space=pl.ANY)],
            out_specs=pl.BlockSpec((rows, D), lambda i, pos: (i, 0)),
            scratch_shapes=[pltpu.VMEM((2, rows, D), F32), pltpu.SemaphoreType.DMA(())],
        ),
        out_shape=jax.ShapeDtypeStruct((T, D), F32),
        compiler_params=_params(("arbitrary",)),
        name="moe_combine",
    )(pos, x1, o)


def _dispatch_plan(route, T):
    eid = route[:, 0:2].astype(jnp.int32)
    gate = route[:, 2:4]
    tk = 2 * T
    e_flat = eid.reshape(-1)
    w_flat = gate.reshape(-1)
    tok_flat = jnp.repeat(jnp.arange(T, dtype=jnp.int32), 2)
    order = jnp.argsort(e_flat)
    se = e_flat[order]
    counts = jnp.bincount(e_flat, length=N_EXP).astype(jnp.int32)
    pcounts = (counts + MOE_ROWS - 1) // MOE_ROWS * MOE_ROWS
    pends = jnp.cumsum(pcounts)
    pstarts = pends - pcounts
    ustarts = jnp.cumsum(counts) - counts
    dest = (pstarts[se] + jnp.arange(tk, dtype=jnp.int32) - ustarts[se]).astype(jnp.int32)
    nb = tk // MOE_ROWS + N_EXP
    slot_tok = jnp.zeros((nb * MOE_ROWS,), jnp.int32).at[dest].set(tok_flat[order])
    slot_w = jnp.zeros((nb * MOE_ROWS,), F32).at[dest].set(w_flat[order])
    blk_exp = jnp.minimum(jnp.searchsorted(pends, jnp.arange(nb, dtype=jnp.int32) * MOE_ROWS, side='right'),
                          N_EXP - 1).astype(jnp.int32)
    nused = (pends[-1:] // MOE_ROWS).astype(jnp.int32)
    pos = jnp.zeros((tk,), jnp.int32).at[order].set(dest)
    return slot_tok, slot_w, blk_exp, nused, pos


def _tile(n, pref):
    return pref if n % pref == 0 else n


def _trunk(x, p):
    B, S, D = x.shape
    T = B * S
    xt = x.reshape(T, D)
    tm = _tile(T, 1024)
    tr = _tile(T, 256)

    xn = _rmsnorm(xt, p["norm_mix_g"], tr)
    qk = _proj(xn, p["w_main"], 2 * W_A, BF16, tm, 512, col_off=0, vec=p["qk_gain"], kind="headnorm",
               name="proj_qk")
    vqkv = _proj(xn, p["w_main"], W_A + 2 * QK_B + W_B, BF16, tm, 512, col_off=2 * W_A, name="proj_vqkv")
    om = _proj(xn, p["w_main"], W_B, F32, tm, 512, col_off=3 * W_A + 2 * QK_B + W_B, name="proj_o")
    gates = _proj(xn, p["w_gate"], 2 * D_MODEL, F32, tm, 512, name="proj_gates")
    gif = _proj(xn, p["w_gif"], LANES, F32, tm, LANES, vec=p["b_gif"], kind="bias", name="proj_gif")

    ta = _tile(S, 512)
    ya = _diff_attention(qk, vqkv, p["slopes"], p["lam"], p["subln_scaled"], B, S, ta, ta)

    g4 = gif[:, :4 * N_HEADS_B].reshape(B, S, 4, N_HEADS_B)
    gcol = g4.transpose(0, 3, 1, 2)
    grow = g4.transpose(0, 3, 2, 1)
    hf, hb = _mlstm(vqkv, gcol, grow, B, S)
    yb = _mlstm_out(hf, hb, om, p["mlstm_norm_g"], tr)

    merged = _merge(ya, yb, p["w_branch_a"], p["w_branch_b"], gates, tm, 512)
    x1 = _outproj(merged, p["w_out"], xt, tm, 512)

    xn2, route = _router(x1, p["norm_ffn_g"], p["w_router"], p["b_router"], tr)
    slot_tok, slot_w, blk_exp, nused, pos = _dispatch_plan(route, T)
    xs = _gather_rows(xn2, slot_tok, MOE_ROWS)
    o = _moe_mlp(xs, p["w1"], p["w3"], p["w2"], slot_w, blk_exp, nused)
    y = _combine(x1, o, pos, _tile(T, 128))
    return y.reshape(B, S, D)


def _prepare(norm_mix_g, w_in, b_mgate, attn_q_norm_g, attn_k_norm_g, lambda_q1, lambda_k1, lambda_q2,
             lambda_k2, attn_subln_g, mlstm_norm_g, w_branch_a, w_branch_b, w_out, norm_ffn_g,
             w_router_group, b_router_group, w_router_expert, b_router_expert, w_exp_gate, w_exp_up,
             w_exp_down):
    l = 0
    n_main = 3 * W_A + 2 * QK_B + 2 * W_B
    n_if = 4 * N_HEADS_B
    w = w_in[l]
    q_gain = jnp.tile((attn_q_norm_g[l].astype(F32) * (HEAD_DIM_A ** -0.5)).reshape(-1), N_HEADS_A)
    k_gain = jnp.tile(attn_k_norm_g[l].astype(F32).reshape(-1), N_HEADS_A)
    lam = (jnp.exp(jnp.sum(lambda_q1[l].astype(F32) * lambda_k1[l].astype(F32)))
           - jnp.exp(jnp.sum(lambda_q2[l].astype(F32) * lambda_k2[l].astype(F32))) + LAM_INIT)
    slopes = 2.0 ** (-8.0 * (jnp.arange(N_HEADS_A, dtype=F32) + 1.0) / N_HEADS_A)
    w_router = jnp.zeros((D_MODEL, LANES), F32)
    w_router = w_router.at[:, :N_GROUPS].set(w_router_group[l])
    w_router = w_router.at[:, N_GROUPS:N_GROUPS + N_EXP].set(w_router_expert[l])
    b_router = jnp.zeros((1, LANES), F32)
    b_router = b_router.at[0, :N_GROUPS].set(b_router_group[l].astype(F32))
    b_router = b_router.at[0, N_GROUPS:N_GROUPS + N_EXP].set(b_router_expert[l].astype(F32).reshape(-1))
    return {
        "norm_mix_g": norm_mix_g[l],
        "w_main": w[:, :n_main].astype(BF16),
        "w_gate": w[:, n_main + n_if:].astype(BF16),
        "w_gif": jnp.pad(w[:, n_main:n_main + n_if], ((0, 0), (0, LANES - n_if))).astype(BF16),
        "b_gif": jnp.pad(b_mgate[l].astype(F32), (0, LANES - n_if)),
        "qk_gain": jnp.concatenate([q_gain, k_gain]),
        "lam": lam.reshape(1),
        "slopes": slopes,
        "subln_scaled": attn_subln_g[l].astype(F32) * (1.0 - LAM_INIT),
        "mlstm_norm_g": mlstm_norm_g[l],
        "w_branch_a": w_branch_a[l].astype(BF16),
        "w_branch_b": w_branch_b[l].astype(BF16),
        "w_out": w_out[l].astype(BF16),
        "norm_ffn_g": norm_ffn_g[l],
        "w_router": w_router,
        "b_router": b_router,
        "w1": w_exp_gate[l].astype(BF16),
        "w3": w_exp_up[l].astype(BF16),
        "w2": w_exp_down[l].astype(BF16),
    }


def kernel(x_prompt, x_sample, norm_mix_g, w_in, b_mgate, attn_q_norm_g, attn_k_norm_g, lambda_q1, lambda_k1, lambda_q2, lambda_k2, attn_subln_g, mlstm_norm_g, w_branch_a, w_branch_b, w_out, norm_ffn_g, w_router_group, b_router_group, w_router_expert, b_router_expert, w_exp_gate, w_exp_up, w_exp_down):
    p = _prepare(norm_mix_g, w_in, b_mgate, attn_q_norm_g, attn_k_norm_g, lambda_q1, lambda_k1, lambda_q2,
                 lambda_k2, attn_subln_g, mlstm_norm_g, w_branch_a, w_branch_b, w_out, norm_ffn_g,
                 w_router_group, b_router_group, w_router_expert, b_router_expert, w_exp_gate, w_exp_up,
                 w_exp_down)
    return (_trunk(x_prompt, p), _trunk(x_sample, p))
```

```python
import functools

import jax
import jax.numpy as jnp
from jax import lax
from jax.experimental import pallas as pl
from jax.experimental.pallas import tpu as pltpu

D_MODEL = 4096
HEAD_DIM_A = 128
N_HEADS_A = 8
W_A = 2048
N_HEADS_B = 4
DV_B = 512
DK_B = 256
W_B = 2048
QK_B = 1024
CHUNK = 128
N_GROUPS = 4
EXP_PER_GROUP = 8
N_EXP = 32
D_FF_EXP = 1024
EPS = 1e-6
NEG = -1e30
LAM_INIT = 0.8 - 0.6 * 1.0

LANES = 128
MOE_ROWS = 256
FF_CHUNK = 512
VMEM_LIMIT = 56 * 1024 * 1024
ATTN_ROWS = 16
ATTN_UNROLL = 4
LOG2E = 1.4426950408889634

F32 = jnp.float32
BF16 = jnp.bfloat16
HIGHEST = lax.Precision.HIGHEST


def _params(sem, vmem=VMEM_LIMIT):
    return pltpu.CompilerParams(dimension_semantics=sem, vmem_limit_bytes=vmem)


def _rmsnorm_kernel(x_ref, g_ref, o_ref):
    x = x_ref[...]
    ms = jnp.mean(x * x, axis=-1, keepdims=True)
    o_ref[...] = (x * lax.rsqrt(ms + EPS) * g_ref[...]).astype(o_ref.dtype)


def _rmsnorm(x, g, tm):
    T, D = x.shape
    return pl.pallas_call(
        _rmsnorm_kernel,
        grid=(T // tm,),
        in_specs=[pl.BlockSpec((tm, D), lambda i: (i, 0)), pl.BlockSpec((1, D), lambda i: (0, 0))],
        out_specs=pl.BlockSpec((tm, D), lambda i: (i, 0)),
        out_shape=jax.ShapeDtypeStruct((T, D), BF16),
        compiler_params=_params(("parallel",)),
        name="rmsnorm",
    )(x, g.reshape(1, D))


def _proj_kernel(a_ref, w_ref, o_ref):
    o_ref[...] = jnp.dot(a_ref[...], w_ref[...], preferred_element_type=F32).astype(o_ref.dtype)


def _proj_bias_kernel(a_ref, w_ref, b_ref, o_ref):
    acc = jnp.dot(a_ref[...], w_ref[...], preferred_element_type=F32)
    o_ref[...] = (acc + b_ref[...]).astype(o_ref.dtype)


def _proj_headnorm_kernel(a_ref, w_ref, g_ref, o_ref):
    acc = jnp.dot(a_ref[...], w_ref[...], preferred_element_type=F32)
    for j in range(acc.shape[1] // HEAD_DIM_A):
        sl = slice(j * HEAD_DIM_A, (j + 1) * HEAD_DIM_A)
        blk = acc[:, sl]
        ms = jnp.mean(blk * blk, axis=-1, keepdims=True)
        o_ref[:, sl] = (blk * lax.rsqrt(ms + EPS) * g_ref[:, sl]).astype(o_ref.dtype)


def _proj(a, w, n_out, out_dtype, tm, tn, *, col_off=0, vec=None, kind="plain", name="proj"):
    T, K = a.shape
    assert n_out % tn == 0 and col_off % tn == 0 and T % tm == 0
    off = col_off // tn
    in_specs = [pl.BlockSpec((tm, K), lambda i, j: (i, 0)),
                pl.BlockSpec((K, tn), lambda i, j: (0, j + off))]
    args = [a, w]
    if kind == "plain":
        body = _proj_kernel
    else:
        body = _proj_bias_kernel if kind == "bias" else _proj_headnorm_kernel
        in_specs.append(pl.BlockSpec((1, tn), lambda i, j: (0, j)))
        args.append(vec.reshape(1, n_out).astype(F32))
    return pl.pallas_call(
        body,
        grid=(T // tm, n_out // tn),
        in_specs=in_specs,
        out_specs=pl.BlockSpec((tm, tn), lambda i, j: (i, j)),
        out_shape=jax.ShapeDtypeStruct((T, n_out), out_dtype),
        compiler_params=_params(("parallel", "arbitrary")),
        name=name,
    )(*args)


def _attn_kernel(slopes_ref, lam_ref, q_ref, k_ref, v_ref, g_ref, o_ref,
                 s_sc, p_sc, pm_sc, ps_sc, e_sc, m_sc, l_sc, acc_sc, *, tq, tk):
    h = pl.program_id(1)
    qi = pl.program_id(2)
    ki = pl.program_id(3)
    rows_per = ATTN_ROWS
    n_lane_tiles = tk // LANES

    @pl.when(ki == 0)
    def _():
        m_sc[...] = jnp.full(m_sc.shape, NEG, F32)
        l_sc[...] = jnp.zeros(l_sc.shape, F32)
        acc_sc[...] = jnp.zeros(acc_sc.shape, F32)

    slope = slopes_ref[h]
    off = qi * tq - ki * tk
    left = off >= tk - 1
    right = off <= 1 - tq
    separable = left | right
    col = lax.broadcasted_iota(jnp.int32, (1, tk), 1)
    row = lax.broadcasted_iota(jnp.int32, (rows_per, 1), 0)
    all_rows = lax.broadcasted_iota(jnp.int32, (tq, 1), 0)
    col_bias = slope * jnp.where(left, tk - 1 - col, col).astype(F32)
    row_term = jnp.where(left, all_rows + (off - (tk - 1)), -off - all_rows).astype(F32)
    shift = jnp.where(separable, -slope * row_term, 0.0)

    def lane_tile(j):
        return slice(j * LANES, (j + 1) * LANES)

    def pass_a(is_separable):
        def body(i, carry):
            r0 = pl.multiple_of(i * rows_per, rows_per)
            rows = pl.ds(r0, rows_per)
            if is_separable:
                t = s_sc[rows, :] - col_bias
            else:
                d = (row + (r0 + off)) - col
                t = s_sc[rows, :] - slope * jnp.abs(d.astype(F32))
                s_sc[rows, :] = t
            pm = t[:, lane_tile(0)]
            for j in range(1, n_lane_tiles):
                pm = jnp.maximum(pm, t[:, lane_tile(j)])
            pm_sc[rows, :] = pm
            return carry

        lax.fori_loop(0, tq // rows_per, body, 0, unroll=ATTN_UNROLL)

    def pass_b(is_separable):
        def body(i, carry):
            r0 = pl.multiple_of(i * rows_per, rows_per)
            rows = pl.ds(r0, rows_per)
            e = e_sc[rows, :]
            ps = None
            for j in range(n_lane_tiles):
                t = s_sc[rows, lane_tile(j)]
                if is_separable:
                    t = t - col_bias[:, lane_tile(j)]
                p = jnp.exp2(t - e)
                p_sc[rows, lane_tile(j)] = p.astype(BF16)
                ps = p if ps is None else ps + p
            ps_sc[rows, :] = ps
            return carry

        lax.fori_loop(0, tq // rows_per, body, 0, unroll=ATTN_UNROLL)

    def both(fn):
        @pl.when(separable)
        def _():
            fn(True)

        @pl.when(jnp.logical_not(separable))
        def _():
            fn(False)

    v = v_ref[...]
    for c in range(2):
        sl = slice(c * HEAD_DIM_A, (c + 1) * HEAD_DIM_A)
        s_sc[...] = lax.dot_general(q_ref[:, sl], k_ref[:, sl], (((1,), (1,)), ((), ())),
                                    preferred_element_type=F32)
        both(pass_a)
        m_prev = m_sc[c]
        m_new = jnp.maximum(m_prev, jnp.max(pm_sc[...], axis=-1, keepdims=True) + shift)
        alpha = jnp.exp2(m_prev - m_new)
        m_sc[c] = m_new
        e_sc[...] = jnp.broadcast_to(m_new - shift, e_sc.shape)
        both(pass_b)
        l_sc[c] = alpha * l_sc[c] + jnp.sum(ps_sc[...], axis=-1, keepdims=True)
        acc_sc[c] = alpha * acc_sc[c] + jnp.dot(p_sc[...], v, preferred_element_type=F32)

    @pl.when(ki == pl.num_programs(3) - 1)
    def _():
        o = acc_sc[0] / l_sc[0] - lam_ref[0] * (acc_sc[1] / l_sc[1])
        ms = jnp.mean(o * o, axis=-1, keepdims=True)
        o_ref[...] = (o * lax.rsqrt(ms + EPS) * g_ref[...]).astype(o_ref.dtype)


def _diff_attention(qk, vqkv, slopes, lam, subln_scaled, B, S, tq, tk):
    T = B * S
    nq, nk = S // tq, S // tk
    hw = 2 * HEAD_DIM_A
    return pl.pallas_call(
        functools.partial(_attn_kernel, tq=tq, tk=tk),
        grid=(B, N_HEADS_A, nq, nk),
        in_specs=[
            pl.BlockSpec(memory_space=pltpu.SMEM),
            pl.BlockSpec(memory_space=pltpu.SMEM),
            pl.BlockSpec((tq, hw), lambda b, h, i, j: (b * nq + i, h)),
            pl.BlockSpec((tk, hw), lambda b, h, i, j: (b * nk + j, N_HEADS_A + h)),
            pl.BlockSpec((tk, hw), lambda b, h, i, j: (b * nk + j, h)),
            pl.BlockSpec((1, hw), lambda b, h, i, j: (0, 0)),
        ],
        out_specs=pl.BlockSpec((tq, hw), lambda b, h, i, j: (b * nq + i, h)),
        out_shape=jax.ShapeDtypeStruct((T, W_A), BF16),
        scratch_shapes=[pltpu.VMEM((tq, tk), F32), pltpu.VMEM((tq, tk), BF16),
                        pltpu.VMEM((tq, LANES), F32), pltpu.VMEM((tq, LANES), F32),
                        pltpu.VMEM((tq, LANES), F32),
                        pltpu.VMEM((2, tq, 1), F32), pltpu.VMEM((2, tq, 1), F32),
                        pltpu.VMEM((2, tq, hw), F32)],
        compiler_params=_params(("parallel", "parallel", "parallel", "arbitrary")),
        name="diff_attention",
    )(slopes, lam, qk, qk, vqkv, subln_scaled.reshape(1, hw))


def _log_sigmoid(x):
    return jnp.minimum(x, 0.0) - jnp.log1p(jnp.exp(-jnp.abs(x)))


def _mlstm_chunk(q, k, v, gcol, grow, c_ref, n_ref, m_ref, reverse):
    L = q.shape[0]
    ii, fi = (2, 3) if reverse else (0, 1)
    r = lax.broadcasted_iota(jnp.int32, (L, L), 0)
    c = lax.broadcasted_iota(jnp.int32, (L, L), 1)
    lower = c <= r
    upper = c >= r
    tri_lo = jnp.where(lower, 1.0, 0.0).astype(F32)
    tri_up = jnp.where(upper, 1.0, 0.0).astype(F32)
    ic_row = grow[ii:ii + 1, :]
    ic_col = gcol[:, ii:ii + 1]
    lf_row = _log_sigmoid(grow[fi:fi + 1, :])
    lf_col = _log_sigmoid(gcol[:, fi:fi + 1])
    if reverse:
        b_row = jnp.dot(lf_row, tri_lo, precision=HIGHEST, preferred_element_type=F32)
        b_col = jnp.dot(tri_up, lf_col, precision=HIGHEST, preferred_element_type=F32)
        mask = upper
        b_last = b_row[:, 0:1]
    else:
        b_row = jnp.dot(lf_row, tri_up, precision=HIGHEST, preferred_element_type=F32)
        b_col = jnp.dot(tri_lo, lf_col, precision=HIGHEST, preferred_element_type=F32)
        mask = lower
        b_last = b_row[:, L - 1:L]
    m = m_ref[...]
    log_d = jnp.where(mask, b_col - b_row + ic_row, NEG)
    inter = b_col + m
    m_row = jnp.maximum(inter, jnp.max(log_d, axis=-1, keepdims=True))
    d_m = jnp.exp(log_d - m_row)
    sc = jnp.exp(inter - m_row)
    qs = q * (DK_B ** -0.5)
    s = lax.dot_general(qs, k, (((1,), (1,)), ((), ())), preferred_element_type=F32) * d_m
    c_state = c_ref[...]
    n_state = n_ref[...]
    num = (jnp.dot(s.astype(BF16), v, preferred_element_type=F32)
           + sc * jnp.dot(qs, c_state.astype(BF16), preferred_element_type=F32))
    den = (jnp.sum(s, axis=-1, keepdims=True)
           + sc * jnp.sum(qs.astype(F32) * n_state, axis=-1, keepdims=True))
    h = num / jnp.maximum(jnp.abs(den), jnp.exp(-m_row))
    w_col = b_last - b_col + ic_col
    w_row = b_last - b_row + ic_row
    m_new = jnp.maximum(b_last + m, jnp.max(w_row, axis=-1, keepdims=True))
    decay = jnp.exp(b_last + m - m_new)
    ws_col = jnp.exp(w_col - m_new)
    wv = (ws_col * v.astype(F32)).astype(BF16)
    c_ref[...] = decay * c_state + lax.dot_general(k, wv, (((0,), (0,)), ((), ())),
                                                   preferred_element_type=F32)
    n_ref[...] = decay * n_state + jnp.sum(ws_col * k.astype(F32), axis=0, keepdims=True)
    m_ref[...] = m_new
    return h


def _mlstm_kernel(qf_ref, kf_ref, vf_ref, gcf_ref, grf_ref, qb_ref, kb_ref, vb_ref, gcb_ref, grb_ref,
                  hf_ref, hb_ref, c_sc, n_sc, m_sc):
    @pl.when(pl.program_id(2) == 0)
    def _():
        c_sc[...] = jnp.zeros(c_sc.shape, F32)
        n_sc[...] = jnp.zeros(n_sc.shape, F32)
        m_sc[...] = jnp.full(m_sc.shape, NEG, F32)

    hf_ref[...] = _mlstm_chunk(qf_ref[...], kf_ref[...], vf_ref[...], gcf_ref[...], grf_ref[...],
                               c_sc.at[0], n_sc.at[0], m_sc.at[0], reverse=False)
    hb_ref[...] = _mlstm_chunk(qb_ref[...], kb_ref[...], vb_ref[...], gcb_ref[...], grb_ref[...],
                               c_sc.at[1], n_sc.at[1], m_sc.at[1], reverse=True)


def _mlstm(vqkv, gcol, grow, B, S):
    T = B * S
    nc = S // CHUNK
    q_off = W_A // DK_B
    k_off = (W_A + QK_B) // DK_B
    v_off = (W_A + 2 * QK_B) // DV_B

    def fwd(b, h, c):
        return b * nc + c

    def bwd(b, h, c):
        return b * nc + nc - 1 - c

    def specs(pos):
        return [
            pl.BlockSpec((CHUNK, DK_B), lambda b, h, c: (pos(b, h, c), q_off + h)),
            pl.BlockSpec((CHUNK, DK_B), lambda b, h, c: (pos(b, h, c), k_off + h)),
            pl.BlockSpec((CHUNK, DV_B), lambda b, h, c: (pos(b, h, c), v_off + h)),
            pl.BlockSpec((None, None, CHUNK, 4), lambda b, h, c: (b, h, pos(b, h, c) - b * nc, 0)),
            pl.BlockSpec((None, None, 4, CHUNK), lambda b, h, c: (b, h, 0, pos(b, h, c) - b * nc)),
        ]

    out_sds = jax.ShapeDtypeStruct((T, W_B), F32)
    return pl.pallas_call(
        _mlstm_kernel,
        grid=(B, N_HEADS_B, nc),
        in_specs=specs(fwd) + specs(bwd),
        out_specs=[pl.BlockSpec((CHUNK, DV_B), lambda b, h, c: (fwd(b, h, c), h)),
                   pl.BlockSpec((CHUNK, DV_B), lambda b, h, c: (bwd(b, h, c), h))],
        out_shape=[out_sds, out_sds],
        scratch_shapes=[pltpu.VMEM((2, DK_B, DV_B), F32), pltpu.VMEM((2, 1, DK_B), F32),
                        pltpu.VMEM((2, 1, 1), F32)],
        compiler_params=_params(("parallel", "parallel", "arbitrary")),
        name="mlstm",
    )(vqkv, vqkv, vqkv, gcol, grow, vqkv, vqkv, vqkv, gcol, grow)


def _mlstm_out_kernel(hf_ref, hb_ref, o_ref, g_ref, y_ref):
    for j in range(N_HEADS_B):
        sl = slice(j * DV_B, (j + 1) * DV_B)
        h = hf_ref[:, sl] + hb_ref[:, sl]
        ms = jnp.mean(h * h, axis=-1, keepdims=True)
        hn = h * lax.rsqrt(ms + EPS) * g_ref[...]
        y_ref[:, sl] = (jax.nn.sigmoid(o_ref[:, sl]) * hn).astype(y_ref.dtype)


def _mlstm_out(hf, hb, om, norm_g, tm):
    T = hf.shape[0]
    row = pl.BlockSpec((tm, W_B), lambda i: (i, 0))
    return pl.pallas_call(
        _mlstm_out_kernel,
        grid=(T // tm,),
        in_specs=[row, row, row, pl.BlockSpec((1, DV_B), lambda i: (0, 0))],
        out_specs=row,
        out_shape=jax.ShapeDtypeStruct((T, W_B), BF16),
        compiler_params=_params(("parallel",)),
        name="mlstm_out",
    )(hf, hb, om, norm_g.reshape(1, DV_B))


def _merge_kernel(ya_ref, yb_ref, wa_ref, wb_ref, ga_ref, gb_ref, o_ref):
    pa = jnp.dot(ya_ref[...], wa_ref[...], preferred_element_type=F32)
    pb = jnp.dot(yb_ref[...], wb_ref[...], preferred_element_type=F32)
    o_ref[...] = (jax.nn.sigmoid(ga_ref[...]) * pa + jax.nn.sigmoid(gb_ref[...]) * pb).astype(o_ref.dtype)


def _merge(ya, yb, wa, wb, gates, tm, tn):
    T = ya.shape[0]
    nb = D_MODEL // tn
    return pl.pallas_call(
        _merge_kernel,
        grid=(T // tm, nb),
        in_specs=[pl.BlockSpec((tm, W_A), lambda i, j: (i, 0)),
                  pl.BlockSpec((tm, W_B), lambda i, j: (i, 0)),
                  pl.BlockSpec((W_A, tn), lambda i, j: (0, j)),
                  pl.BlockSpec((W_B, tn), lambda i, j: (0, j)),
                  pl.BlockSpec((tm, tn), lambda i, j: (i, j)),
                  pl.BlockSpec((tm, tn), lambda i, j: (i, nb + j))],
        out_specs=pl.BlockSpec((tm, tn), lambda i, j: (i, j)),
        out_shape=jax.ShapeDtypeStruct((T, D_MODEL), BF16),
        compiler_params=_params(("parallel", "arbitrary")),
        name="merge",
    )(ya, yb, wa, wb, gates, gates)


def _outproj_kernel(a_ref, w_ref, x_ref, o_ref):
    o_ref[...] = x_ref[...] + jnp.dot(a_ref[...], w_ref[...], preferred_element_type=F32)


def _outproj(merged, w, x, tm, tn):
    T = merged.shape[0]
    return pl.pallas_call(
        _outproj_kernel,
        grid=(T // tm, D_MODEL // tn),
        in_specs=[pl.BlockSpec((tm, D_MODEL), lambda i, j: (i, 0)),
                  pl.BlockSpec((D_MODEL, tn), lambda i, j: (0, j)),
                  pl.BlockSpec((tm, tn), lambda i, j: (i, j))],
        out_specs=pl.BlockSpec((tm, tn), lambda i, j: (i, j)),
        out_shape=jax.ShapeDtypeStruct((T, D_MODEL), F32),
        compiler_params=_params(("parallel", "arbitrary")),
        name="outproj",
    )(merged, w, x)


def _router_kernel(x_ref, g_ref, wr_ref, br_ref, xn_ref, r_ref):
    x = x_ref[...]
    ms = jnp.mean(x * x, axis=-1, keepdims=True)
    xn = x * lax.rsqrt(ms + EPS) * g_ref[...]
    xn_ref[...] = xn
    logits = jnp.dot(xn, wr_ref[...], precision=HIGHEST, preferred_element_type=F32) + br_ref[...]
    lane = lax.broadcasted_iota(jnp.int32, logits.shape, 1)
    cm = jnp.where(lane < N_GROUPS, logits, NEG)
    cmax = jnp.max(cm, axis=-1, keepdims=True)
    p_top = 1.0 / jnp.sum(jnp.exp(cm - cmax), axis=-1, keepdims=True)
    grp = jnp.min(jnp.where(cm == cmax, lane, LANES), axis=-1, keepdims=True)
    lo = N_GROUPS + EXP_PER_GROUP * grp
    fm = jnp.where((lane >= lo) & (lane < lo + EXP_PER_GROUP), logits, NEG)
    fmax = jnp.max(fm, axis=-1, keepdims=True)
    fe = jnp.exp(fm - fmax)
    pe = fe / jnp.sum(fe, axis=-1, keepdims=True)
    i1 = jnp.min(jnp.where(fm == fmax, lane, LANES), axis=-1, keepdims=True)
    p1 = jnp.sum(jnp.where(lane == i1, pe, 0.0), axis=-1, keepdims=True)
    fm2 = jnp.where(lane == i1, NEG, fm)
    f2max = jnp.max(fm2, axis=-1, keepdims=True)
    i2 = jnp.min(jnp.where(fm2 == f2max, lane, LANES), axis=-1, keepdims=True)
    p2 = jnp.sum(jnp.where(lane == i2, pe, 0.0), axis=-1, keepdims=True)
    g1 = p_top * (p1 / (p1 + p2))
    g2 = p_top * (p2 / (p1 + p2))
    e1 = (i1 - N_GROUPS).astype(F32)
    e2 = (i2 - N_GROUPS).astype(F32)
    r_ref[...] = jnp.where(lane == 0, e1, jnp.where(lane == 1, e2, jnp.where(lane == 2, g1,
                           jnp.where(lane == 3, g2, 0.0))))


def _router(x1, g, wr, br, tm):
    T = x1.shape[0]
    return pl.pallas_call(
        _router_kernel,
        grid=(T // tm,),
        in_specs=[pl.BlockSpec((tm, D_MODEL), lambda i: (i, 0)),
                  pl.BlockSpec((1, D_MODEL), lambda i: (0, 0)),
                  pl.BlockSpec((D_MODEL, LANES), lambda i: (0, 0)),
                  pl.BlockSpec((1, LANES), lambda i: (0, 0))],
        out_specs=[pl.BlockSpec((tm, D_MODEL), lambda i: (i, 0)),
                   pl.BlockSpec((tm, LANES), lambda i: (i, 0))],
        out_shape=[jax.ShapeDtypeStruct((T, D_MODEL), F32), jax.ShapeDtypeStruct((T, LANES), F32)],
        compiler_params=_params(("parallel",)),
        name="router",
    )(x1, g.reshape(1, D_MODEL), wr, br)


def _row_copy(src_hbm, row, dst, r, sem):
    return pltpu.make_async_copy(src_hbm.at[pl.ds(row, 1)], dst.at[pl.ds(r, 1)], sem)


def _gather_kernel(idx_ref, src_hbm, o_ref, sem, *, rows):
    base = pl.program_id(0) * rows

    def start(r, carry):
        _row_copy(src_hbm, idx_ref[base + r], o_ref, r, sem).start()
        return carry

    def wait(r, carry):
        _row_copy(src_hbm, 0, o_ref, r, sem).wait()
        return carry

    lax.fori_loop(0, rows, start, 0)
    lax.fori_loop(0, rows, wait, 0)


def _gather_rows(src, idx, rows):
    n = idx.shape[0]
    D = src.shape[1]
    return pl.pallas_call(
        functools.partial(_gather_kernel, rows=rows),
        grid_spec=pltpu.PrefetchScalarGridSpec(
            num_scalar_prefetch=1,
            grid=(n // rows,),
            in_specs=[pl.BlockSpec(memory_space=pl.ANY)],
            out_specs=pl.BlockSpec((rows, D), lambda i, idx: (i, 0)),
            scratch_shapes=[pltpu.SemaphoreType.DMA(())],
        ),
        out_shape=jax.ShapeDtypeStruct((n, D), src.dtype),
        compiler_params=_params(("arbitrary",)),
        name="moe_gather",
    )(idx, src)


def _moe_kernel(be_ref, nused_ref, xs_ref, w1_ref, w3_ref, w2_ref, o_ref):
    b = pl.program_id(0)
    f = pl.program_id(1)

    @pl.when(b < nused_ref[0])
    def _():
        x = xs_ref[...].astype(BF16)
        h1 = jnp.dot(x, w1_ref[...], preferred_element_type=F32)
        h3 = jnp.dot(x, w3_ref[...], preferred_element_type=F32)
        hid = (h1 * jax.nn.sigmoid(h1)) * h3
        out = jnp.dot(hid.astype(BF16), w2_ref[...], preferred_element_type=F32)

        @pl.when(f == 0)
        def _():
            o_ref[...] = out

        @pl.when(f > 0)
        def _():
            o_ref[...] += out

    @pl.when((b >= nused_ref[0]) & (f == 0))
    def _():
        o_ref[...] = jnp.zeros(o_ref.shape, o_ref.dtype)


def _moe_mlp(xs, w1, w3, w2, blk_exp, nused):
    ns = xs.shape[0]
    nb = ns // MOE_ROWS
    nf = D_FF_EXP // FF_CHUNK
    return pl.pallas_call(
        _moe_kernel,
        grid_spec=pltpu.PrefetchScalarGridSpec(
            num_scalar_prefetch=2,
            grid=(nb, nf),
            in_specs=[
                pl.BlockSpec((MOE_ROWS, D_MODEL), lambda b, f, be, nu: (b, 0)),
                pl.BlockSpec((None, D_MODEL, FF_CHUNK), lambda b, f, be, nu: (be[b], 0, f)),
                pl.BlockSpec((None, D_MODEL, FF_CHUNK), lambda b, f, be, nu: (be[b], 0, f)),
                pl.BlockSpec((None, FF_CHUNK, D_MODEL), lambda b, f, be, nu: (be[b], f, 0)),
            ],
            out_specs=pl.BlockSpec((MOE_ROWS, D_MODEL), lambda b, f, be, nu: (b, 0)),
        ),
        out_shape=jax.ShapeDtypeStruct((ns, D_MODEL), F32),
        compiler_params=_params(("arbitrary", "arbitrary")),
        name="moe_mlp",
    )(blk_exp, nused, xs, w1, w3, w2)


def _combine_kernel(pos_ref, x_ref, r_ref, o_hbm, y_ref, buf, sem, *, rows):
    base = pl.program_id(0) * rows

    def start(r, carry):
        for k in range(2):
            _row_copy(o_hbm, pos_ref[2 * (base + r) + k], buf.at[k], r, sem).start()
        return carry

    def wait(r, carry):
        for k in range(2):
            _row_copy(o_hbm, 0, buf.at[k], r, sem).wait()
        return carry

    lax.fori_loop(0, rows, start, 0)
    lax.fori_loop(0, rows, wait, 0)
    r = r_ref[...]
    y_ref[...] = x_ref[...] + (buf[0] * r[:, 2:3] + buf[1] * r[:, 3:4])


def _combine(x1, route, o, pos, rows):
    T, D = x1.shape
    return pl.pallas_call(
        functools.partial(_combine_kernel, rows=rows),
        grid_spec=pltpu.PrefetchScalarGridSpec(
            num_scalar_prefetch=1,
            grid=(T // rows,),
            in_specs=[pl.BlockSpec((rows, D), lambda i, pos: (i, 0)),
                      pl.BlockSpec((rows, LANES), lambda i, pos: (i, 0)),
                      pl.BlockSpec(memory_space=pl.ANY)],
            out_specs=pl.BlockSpec((rows, D), lambda i, pos: (i, 0)),
            scratch_shapes=[pltpu.VMEM((2, rows, D), F32), pltpu.SemaphoreType.DMA(())],
        ),
        out_shape=jax.ShapeDtypeStruct((T, D), F32),
        compiler_params=_params(("arbitrary",)),
        name="moe_combine",
    )(pos, x1, route, o)


def _dispatch_plan(route, T):
    i32 = jnp.int32
    tk = 2 * T
    nb = tk // MOE_ROWS + N_EXP
    experts = jnp.arange(N_EXP, dtype=i32)
    e_flat = route[:, 0:2].astype(i32).reshape(-1)
    order = jnp.argsort(e_flat).astype(i32)
    rank = jnp.argsort(order).astype(i32)
    counts = jnp.sum(e_flat[:, None] == experts[None, :], axis=0, dtype=i32)
    pcounts = (counts + MOE_ROWS - 1) // MOE_ROWS * MOE_ROWS
    pends = jnp.cumsum(pcounts)
    pstarts = pends - pcounts
    ustarts = jnp.cumsum(counts) - counts
    pos = pstarts[e_flat] + rank - ustarts[e_flat]
    slot = jnp.arange(nb * MOE_ROWS, dtype=i32)
    slot_exp = jnp.minimum(jnp.sum(pends[None, :] <= slot[:, None], axis=1, dtype=i32), N_EXP - 1)
    within = slot - pstarts[slot_exp]
    src = jnp.clip(ustarts[slot_exp] + within, 0, tk - 1)
    slot_tok = jnp.where(within < counts[slot_exp], order[src] // 2, 0).astype(i32)
    blk_exp = slot_exp[::MOE_ROWS]
    nused = (pends[-1:] // MOE_ROWS).astype(i32)
    return slot_tok, blk_exp, nused, pos.astype(i32)


def _tile(n, pref):
    return pref if n % pref == 0 else n


def _trunk(x, p):
    B, S, D = x.shape
    T = B * S
    xt = x.reshape(T, D)
    tm = _tile(T, 1024)
    tr = _tile(T, 256)

    xn = _rmsnorm(xt, p["norm_mix_g"], tr)
    qk = _proj(xn, p["w_main"], 2 * W_A, BF16, tm, 512, col_off=0, vec=p["qk_gain"], kind="headnorm",
               name="proj_qk")
    vqkv = _proj(xn, p["w_main"], W_A + 2 * QK_B + W_B, BF16, tm, 512, col_off=2 * W_A, name="proj_vqkv")
    om = _proj(xn, p["w_main"], W_B, F32, tm, 512, col_off=3 * W_A + 2 * QK_B + W_B, name="proj_o")
    gates = _proj(xn, p["w_gate"], 2 * D_MODEL, F32, tm, 512, name="proj_gates")
    gif = _proj(xn, p["w_gif"], LANES, F32, tm, LANES, vec=p["b_gif"], kind="bias", name="proj_gif")

    ya = _diff_attention(qk, vqkv, p["slopes"], p["lam"], p["subln_scaled"], B, S,
                         _tile(S, 512), _tile(S, 2048))

    g4 = gif[:, :4 * N_HEADS_B].reshape(B, S, 4, N_HEADS_B)
    gcol = g4.transpose(0, 3, 1, 2)
    grow = g4.transpose(0, 3, 2, 1)
    hf, hb = _mlstm(vqkv, gcol, grow, B, S)
    yb = _mlstm_out(hf, hb, om, p["mlstm_norm_g"], tr)

    merged = _merge(ya, yb, p["w_branch_a"], p["w_branch_b"], gates, tm, 512)
    x1 = _outproj(merged, p["w_out"], xt, tm, 512)

    xn2, route = _router(x1, p["norm_ffn_g"], p["w_router"], p["b_router"], tr)
    slot_tok, blk_exp, nused, pos = _dispatch_plan(route, T)
    xs = _gather_rows(xn2, slot_tok, MOE_ROWS)
    o = _moe_mlp(xs, p["w1"], p["w3"], p["w2"], blk_exp, nused)
    y = _combine(x1, route, o, pos, _tile(T, 128))
    return y.reshape(B, S, D)


def _prepare(norm_mix_g, w_in, b_mgate, attn_q_norm_g, attn_k_norm_g, lambda_q1, lambda_k1, lambda_q2,
             lambda_k2, attn_subln_g, mlstm_norm_g, w_branch_a, w_branch_b, w_out, norm_ffn_g,
             w_router_group, b_router_group, w_router_expert, b_router_expert, w_exp_gate, w_exp_up,
             w_exp_down):
    l = 0
    n_main = 3 * W_A + 2 * QK_B + 2 * W_B
    n_if = 4 * N_HEADS_B
    w = w_in[l]
    q_gain = jnp.tile((attn_q_norm_g[l].astype(F32) * (HEAD_DIM_A ** -0.5 * LOG2E)).reshape(-1), N_HEADS_A)
    k_gain = jnp.tile(attn_k_norm_g[l].astype(F32).reshape(-1), N_HEADS_A)
    lam = (jnp.exp(jnp.sum(lambda_q1[l].astype(F32) * lambda_k1[l].astype(F32)))
           - jnp.exp(jnp.sum(lambda_q2[l].astype(F32) * lambda_k2[l].astype(F32))) + LAM_INIT)
    slopes = LOG2E * 2.0 ** (-8.0 * (jnp.arange(N_HEADS_A, dtype=F32) + 1.0) / N_HEADS_A)
    w_router = jnp.zeros((D_MODEL, LANES), F32)
    w_router = w_router.at[:, :N_GROUPS].set(w_router_group[l])
    w_router = w_router.at[:, N_GROUPS:N_GROUPS + N_EXP].set(w_router_expert[l])
    b_router = jnp.zeros((1, LANES), F32)
    b_router = b_router.at[0, :N_GROUPS].set(b_router_group[l].astype(F32))
    b_router = b_router.at[0, N_GROUPS:N_GROUPS + N_EXP].set(b_router_expert[l].astype(F32).reshape(-1))
    return {
        "norm_mix_g": norm_mix_g[l],
        "w_main": w[:, :n_main].astype(BF16),
        "w_gate": w[:, n_main + n_if:].astype(BF16),
        "w_gif": jnp.pad(w[:, n_main:n_main + n_if], ((0, 0), (0, LANES - n_if))).astype(BF16),
        "b_gif": jnp.pad(b_mgate[l].astype(F32), (0, LANES - n_if)),
        "qk_gain": jnp.concatenate([q_gain, k_gain]),
        "lam": lam.reshape(1),
        "slopes": slopes,
        "subln_scaled": attn_subln_g[l].astype(F32) * (1.0 - LAM_INIT),
        "mlstm_norm_g": mlstm_norm_g[l],
        "w_branch_a": w_branch_a[l].astype(BF16),
        "w_branch_b": w_branch_b[l].astype(BF16),
        "w_out": w_out[l].astype(BF16),
        "norm_ffn_g": norm_ffn_g[l],
        "w_router": w_router,
        "b_router": b_router,
        "w1": w_exp_gate[l].astype(BF16),
        "w3": w_exp_up[l].astype(BF16),
        "w2": w_exp_down[l].astype(BF16),
    }


def kernel(x_prompt, x_sample, norm_mix_g, w_in, b_mgate, attn_q_norm_g, attn_k_norm_g, lambda_q1, lambda_k1, lambda_q2, lambda_k2, attn_subln_g, mlstm_norm_g, w_branch_a, w_branch_b, w_out, norm_ffn_g, w_router_group, b_router_group, w_router_expert, b_router_expert, w_exp_gate, w_exp_up, w_exp_down):
    p = _prepare(norm_mix_g, w_in, b_mgate, attn_q_norm_g, attn_k_norm_g, lambda_q1, lambda_k1, lambda_q2,
                 lambda_k2, attn_subln_g, mlstm_norm_g, w_branch_a, w_branch_b, w_out, norm_ffn_g,
                 w_router_group, b_router_group, w_router_expert, b_router_expert, w_exp_gate, w_exp_up,
                 w_exp_down)
    return (_trunk(x_prompt, p), _trunk(x_sample, p))
```

```python
import functools

import jax
import jax.numpy as jnp
from jax import lax
from jax.experimental import pallas as pl
from jax.experimental.pallas import tpu as pltpu

D_MODEL = 4096
HEAD_DIM_A = 128
N_HEADS_A = 8
W_A = 2048
N_HEADS_B = 4
DV_B = 512
DK_B = 256
W_B = 2048
QK_B = 1024
CHUNK = 128
N_GROUPS = 4
EXP_PER_GROUP = 8
N_EXP = 32
D_FF_EXP = 1024
EPS = 1e-6
NEG = -1e30
LAM_INIT = 0.8 - 0.6 * 1.0

LANES = 128
MOE_ROWS = 256
FF_CHUNK = 512
VMEM_LIMIT = 56 * 1024 * 1024
ATTN_ROWS = 16
ATTN_UNROLL = 4
LOG2E = 1.4426950408889634

F32 = jnp.float32
BF16 = jnp.bfloat16
HIGHEST = lax.Precision.HIGHEST


def _params(sem, vmem=VMEM_LIMIT):
    return pltpu.CompilerParams(dimension_semantics=sem, vmem_limit_bytes=vmem)


def _rmsnorm_kernel(x_ref, g_ref, o_ref):
    x = x_ref[...]
    ms = jnp.mean(x * x, axis=-1, keepdims=True)
    o_ref[...] = (x * lax.rsqrt(ms + EPS) * g_ref[...]).astype(o_ref.dtype)


def _rmsnorm(x, g, tm):
    T, D = x.shape
    return pl.pallas_call(
        _rmsnorm_kernel,
        grid=(T // tm,),
        in_specs=[pl.BlockSpec((tm, D), lambda i: (i, 0)), pl.BlockSpec((1, D), lambda i: (0, 0))],
        out_specs=pl.BlockSpec((tm, D), lambda i: (i, 0)),
        out_shape=jax.ShapeDtypeStruct((T, D), BF16),
        compiler_params=_params(("parallel",)),
        name="rmsnorm",
    )(x, g.reshape(1, D))


def _proj_kernel(a_ref, w_ref, o_ref):
    o_ref[...] = jnp.dot(a_ref[...], w_ref[...], preferred_element_type=F32).astype(o_ref.dtype)


def _proj_bias_kernel(a_ref, w_ref, b_ref, o_ref):
    acc = jnp.dot(a_ref[...], w_ref[...], preferred_element_type=F32)
    o_ref[...] = (acc + b_ref[...]).astype(o_ref.dtype)


def _proj_headnorm_kernel(a_ref, w_ref, g_ref, o_ref):
    acc = jnp.dot(a_ref[...], w_ref[...], preferred_element_type=F32)
    for j in range(acc.shape[1] // HEAD_DIM_A):
        sl = slice(j * HEAD_DIM_A, (j + 1) * HEAD_DIM_A)
        blk = acc[:, sl]
        ms = jnp.mean(blk * blk, axis=-1, keepdims=True)
        o_ref[:, sl] = (blk * lax.rsqrt(ms + EPS) * g_ref[:, sl]).astype(o_ref.dtype)


def _proj(a, w, n_out, out_dtype, tm, tn, *, col_off=0, vec=None, kind="plain", name="proj"):
    T, K = a.shape
    assert n_out % tn == 0 and col_off % tn == 0 and T % tm == 0
    off = col_off // tn
    in_specs = [pl.BlockSpec((tm, K), lambda i, j: (i, 0)),
                pl.BlockSpec((K, tn), lambda i, j: (0, j + off))]
    args = [a, w]
    if kind == "plain":
        body = _proj_kernel
    else:
        body = _proj_bias_kernel if kind == "bias" else _proj_headnorm_kernel
        in_specs.append(pl.BlockSpec((1, tn), lambda i, j: (0, j)))
        args.append(vec.reshape(1, n_out).astype(F32))
    return pl.pallas_call(
        body,
        grid=(T // tm, n_out // tn),
        in_specs=in_specs,
        out_specs=pl.BlockSpec((tm, tn), lambda i, j: (i, j)),
        out_shape=jax.ShapeDtypeStruct((T, n_out), out_dtype),
        compiler_params=_params(("parallel", "arbitrary")),
        name=name,
    )(*args)


def _attn_kernel(slopes_ref, lam_ref, q_ref, k_ref, v_ref, g_ref, o_ref,
                 s_sc, p_sc, pm_sc, ps_sc, e_sc, m_sc, l_sc, acc_sc, *, tq, tk):
    h = pl.program_id(1)
    qi = pl.program_id(2)
    ki = pl.program_id(3)
    rows_per = ATTN_ROWS
    n_lane_tiles = tk // LANES

    @pl.when(ki == 0)
    def _():
        m_sc[...] = jnp.full(m_sc.shape, NEG, F32)
        l_sc[...] = jnp.zeros(l_sc.shape, F32)
        acc_sc[...] = jnp.zeros(acc_sc.shape, F32)

    slope = slopes_ref[h]
    off = qi * tq - ki * tk
    left = off >= tk - 1
    right = off <= 1 - tq
    separable = left | right
    col = lax.broadcasted_iota(jnp.int32, (1, tk), 1)
    row = lax.broadcasted_iota(jnp.int32, (rows_per, 1), 0)
    all_rows = lax.broadcasted_iota(jnp.int32, (tq, 1), 0)
    col_bias = slope * jnp.where(left, tk - 1 - col, col).astype(F32)
    row_term = jnp.where(left, all_rows + (off - (tk - 1)), -off - all_rows).astype(F32)
    shift = jnp.where(separable, -slope * row_term, 0.0)

    def lane_tile(j):
        return slice(j * LANES, (j + 1) * LANES)

    def row_group(i):
        return slice(i * rows_per, (i + 1) * rows_per)

    def pass_a(c, is_separable):
        for i in range(tq // rows_per):
            rows = row_group(i)
            if is_separable:
                t = s_sc[c, rows, :] - col_bias
            else:
                d = (row + (i * rows_per + off)) - col
                t = s_sc[c, rows, :] - slope * jnp.abs(d.astype(F32))
                s_sc[c, rows, :] = t
            pm = t[:, lane_tile(0)]
            for j in range(1, n_lane_tiles):
                pm = jnp.maximum(pm, t[:, lane_tile(j)])
            pm_sc[c, rows, :] = pm

    def pass_b(c, is_separable):
        for i in range(tq // rows_per):
            rows = row_group(i)
            e = e_sc[c, rows, :]
            ps = None
            for j in range(n_lane_tiles):
                t = s_sc[c, rows, lane_tile(j)]
                if is_separable:
                    t = t - col_bias[:, lane_tile(j)]
                p = jnp.exp2(t - e)
                p_sc[c, rows, lane_tile(j)] = p.astype(BF16)
                ps = p if ps is None else ps + p
            ps_sc[c, rows, :] = ps

    def tile(is_separable):
        v = v_ref[...]
        for c in range(2):
            sl = slice(c * HEAD_DIM_A, (c + 1) * HEAD_DIM_A)
            s_sc[c] = lax.dot_general(q_ref[:, sl], k_ref[:, sl], (((1,), (1,)), ((), ())),
                                      preferred_element_type=F32)
        for c in range(2):
            pass_a(c, is_separable)
            m_prev = m_sc[c]
            m_new = jnp.maximum(m_prev, jnp.max(pm_sc[c], axis=-1, keepdims=True) + shift)
            alpha = jnp.exp2(m_prev - m_new)
            m_sc[c] = m_new
            e_sc[c] = jnp.broadcast_to(m_new - shift, (tq, LANES))
            pass_b(c, is_separable)
            l_sc[c] = alpha * l_sc[c] + jnp.sum(ps_sc[c], axis=-1, keepdims=True)
            acc_sc[c] = alpha * acc_sc[c] + jnp.dot(p_sc[c], v, preferred_element_type=F32)

    @pl.when(separable)
    def _():
        tile(True)

    @pl.when(jnp.logical_not(separable))
    def _():
        tile(False)

    @pl.when(ki == pl.num_programs(3) - 1)
    def _():
        o = acc_sc[0] / l_sc[0] - lam_ref[0] * (acc_sc[1] / l_sc[1])
        ms = jnp.mean(o * o, axis=-1, keepdims=True)
        o_ref[...] = (o * lax.rsqrt(ms + EPS) * g_ref[...]).astype(o_ref.dtype)


def _diff_attention(qk, vqkv, slopes, lam, subln_scaled, B, S, tq, tk):
    T = B * S
    nq, nk = S // tq, S // tk
    hw = 2 * HEAD_DIM_A
    return pl.pallas_call(
        functools.partial(_attn_kernel, tq=tq, tk=tk),
        grid=(B, N_HEADS_A, nq, nk),
        in_specs=[
            pl.BlockSpec(memory_space=pltpu.SMEM),
            pl.BlockSpec(memory_space=pltpu.SMEM),
            pl.BlockSpec((tq, hw), lambda b, h, i, j: (b * nq + i, h)),
            pl.BlockSpec((tk, hw), lambda b, h, i, j: (b * nk + j, N_HEADS_A + h)),
            pl.BlockSpec((tk, hw), lambda b, h, i, j: (b * nk + j, h)),
            pl.BlockSpec((1, hw), lambda b, h, i, j: (0, 0)),
        ],
        out_specs=pl.BlockSpec((tq, hw), lambda b, h, i, j: (b * nq + i, h)),
        out_shape=jax.ShapeDtypeStruct((T, W_A), BF16),
        scratch_shapes=[pltpu.VMEM((2, tq, tk), F32), pltpu.VMEM((2, tq, tk), BF16),
                        pltpu.VMEM((2, tq, LANES), F32), pltpu.VMEM((2, tq, LANES), F32),
                        pltpu.VMEM((2, tq, LANES), F32),
                        pltpu.VMEM((2, tq, 1), F32), pltpu.VMEM((2, tq, 1), F32),
                        pltpu.VMEM((2, tq, hw), F32)],
        compiler_params=_params(("parallel", "parallel", "parallel", "arbitrary")),
        name="diff_attention",
    )(slopes, lam, qk, qk, vqkv, subln_scaled.reshape(1, hw))


def _log_sigmoid(x):
    return jnp.minimum(x, 0.0) - jnp.log1p(jnp.exp(-jnp.abs(x)))


def _mlstm_chunk(q, k, v, gcol, grow, c_ref, n_ref, m_ref, reverse):
    L = q.shape[0]
    ii, fi = (2, 3) if reverse else (0, 1)
    r = lax.broadcasted_iota(jnp.int32, (L, L), 0)
    c = lax.broadcasted_iota(jnp.int32, (L, L), 1)
    lower = c <= r
    upper = c >= r
    tri_lo = jnp.where(lower, 1.0, 0.0).astype(F32)
    tri_up = jnp.where(upper, 1.0, 0.0).astype(F32)
    ic_row = grow[ii:ii + 1, :]
    ic_col = gcol[:, ii:ii + 1]
    lf_row = _log_sigmoid(grow[fi:fi + 1, :])
    lf_col = _log_sigmoid(gcol[:, fi:fi + 1])
    if reverse:
        b_row = jnp.dot(lf_row, tri_lo, precision=HIGHEST, preferred_element_type=F32)
        b_col = jnp.dot(tri_up, lf_col, precision=HIGHEST, preferred_element_type=F32)
        mask = upper
        b_last = b_row[:, 0:1]
    else:
        b_row = jnp.dot(lf_row, tri_up, precision=HIGHEST, preferred_element_type=F32)
        b_col = jnp.dot(tri_lo, lf_col, precision=HIGHEST, preferred_element_type=F32)
        mask = lower
        b_last = b_row[:, L - 1:L]
    m = m_ref[...]
    log_d = jnp.where(mask, b_col - b_row + ic_row, NEG)
    inter = b_col + m
    m_row = jnp.maximum(inter, jnp.max(log_d, axis=-1, keepdims=True))
    d_m = jnp.exp(log_d - m_row)
    sc = jnp.exp(inter - m_row)
    qs = q * (DK_B ** -0.5)
    s = lax.dot_general(qs, k, (((1,), (1,)), ((), ())), preferred_element_type=F32) * d_m
    c_state = c_ref[...]
    n_state = n_ref[...]
    num = (jnp.dot(s.astype(BF16), v, preferred_element_type=F32)
           + sc * jnp.dot(qs, c_state.astype(BF16), preferred_element_type=F32))
    den = (jnp.sum(s, axis=-1, keepdims=True)
           + sc * jnp.sum(qs.astype(F32) * n_state, axis=-1, keepdims=True))
    h = num / jnp.maximum(jnp.abs(den), jnp.exp(-m_row))
    w_col = b_last - b_col + ic_col
    w_row = b_last - b_row + ic_row
    m_new = jnp.maximum(b_last + m, jnp.max(w_row, axis=-1, keepdims=True))
    decay = jnp.exp(b_last + m - m_new)
    ws_col = jnp.exp(w_col - m_new)
    wv = (ws_col * v.astype(F32)).astype(BF16)
    c_ref[...] = decay * c_state + lax.dot_general(k, wv, (((0,), (0,)), ((), ())),
                                                   preferred_element_type=F32)
    n_ref[...] = decay * n_state + jnp.sum(ws_col * k.astype(F32), axis=0, keepdims=True)
    m_ref[...] = m_new
    return h


def _mlstm_kernel(qf_ref, kf_ref, vf_ref, gcf_ref, grf_ref, qb_ref, kb_ref, vb_ref, gcb_ref, grb_ref,
                  hf_ref, hb_ref, c_sc, n_sc, m_sc):
    @pl.when(pl.program_id(2) == 0)
    def _():
        c_sc[...] = jnp.zeros(c_sc.shape, F32)
        n_sc[...] = jnp.zeros(n_sc.shape, F32)
        m_sc[...] = jnp.full(m_sc.shape, NEG, F32)

    hf_ref[...] = _mlstm_chunk(qf_ref[...], kf_ref[...], vf_ref[...], gcf_ref[...], grf_ref[...],
                               c_sc.at[0], n_sc.at[0], m_sc.at[0], reverse=False)
    hb_ref[...] = _mlstm_chunk(qb_ref[...], kb_ref[...], vb_ref[...], gcb_ref[...], grb_ref[...],
                               c_sc.at[1], n_sc.at[1], m_sc.at[1], reverse=True)


def _mlstm(vqkv, gcol, grow, B, S):
    T = B * S
    nc = S // CHUNK
    q_off = W_A // DK_B
    k_off = (W_A + QK_B) // DK_B
    v_off = (W_A + 2 * QK_B) // DV_B

    def fwd(b, h, c):
        return b * nc + c

    def bwd(b, h, c):
        return b * nc + nc - 1 - c

    def specs(pos):
        return [
            pl.BlockSpec((CHUNK, DK_B), lambda b, h, c: (pos(b, h, c), q_off + h)),
            pl.BlockSpec((CHUNK, DK_B), lambda b, h, c: (pos(b, h, c), k_off + h)),
            pl.BlockSpec((CHUNK, DV_B), lambda b, h, c: (pos(b, h, c), v_off + h)),
            pl.BlockSpec((None, None, CHUNK, 4), lambda b, h, c: (b, h, pos(b, h, c) - b * nc, 0)),
            pl.BlockSpec((None, None, 4, CHUNK), lambda b, h, c: (b, h, 0, pos(b, h, c) - b * nc)),
        ]

    out_sds = jax.ShapeDtypeStruct((T, W_B), F32)
    return pl.pallas_call(
        _mlstm_kernel,
        grid=(B, N_HEADS_B, nc),
        in_specs=specs(fwd) + specs(bwd),
        out_specs=[pl.BlockSpec((CHUNK, DV_B), lambda b, h, c: (fwd(b, h, c), h)),
                   pl.BlockSpec((CHUNK, DV_B), lambda b, h, c: (bwd(b, h, c), h))],
        out_shape=[out_sds, out_sds],
        scratch_shapes=[pltpu.VMEM((2, DK_B, DV_B), F32), pltpu.VMEM((2, 1, DK_B), F32),
                        pltpu.VMEM((2, 1, 1), F32)],
        compiler_params=_params(("parallel", "parallel", "arbitrary")),
        name="mlstm",
    )(vqkv, vqkv, vqkv, gcol, grow, vqkv, vqkv, vqkv, gcol, grow)


def _mlstm_out_kernel(hf_ref, hb_ref, o_ref, g_ref, y_ref):
    for j in range(N_HEADS_B):
        sl = slice(j * DV_B, (j + 1) * DV_B)
        h = hf_ref[:, sl] + hb_ref[:, sl]
        ms = jnp.mean(h * h, axis=-1, keepdims=True)
        hn = h * lax.rsqrt(ms + EPS) * g_ref[...]
        y_ref[:, sl] = (jax.nn.sigmoid(o_ref[:, sl]) * hn).astype(y_ref.dtype)


def _mlstm_out(hf, hb, om, norm_g, tm):
    T = hf.shape[0]
    row = pl.BlockSpec((tm, W_B), lambda i: (i, 0))
    return pl.pallas_call(
        _mlstm_out_kernel,
        grid=(T // tm,),
        in_specs=[row, row, row, pl.BlockSpec((1, DV_B), lambda i: (0, 0))],
        out_specs=row,
        out_shape=jax.ShapeDtypeStruct((T, W_B), BF16),
        compiler_params=_params(("parallel",)),
        name="mlstm_out",
    )(hf, hb, om, norm_g.reshape(1, DV_B))


def _merge_kernel(ya_ref, yb_ref, wa_ref, wb_ref, ga_ref, gb_ref, o_ref):
    pa = jnp.dot(ya_ref[...], wa_ref[...], preferred_element_type=F32)
    pb = jnp.dot(yb_ref[...], wb_ref[...], preferred_element_type=F32)
    o_ref[...] = (jax.nn.sigmoid(ga_ref[...]) * pa + jax.nn.sigmoid(gb_ref[...]) * pb).astype(o_ref.dtype)


def _merge(ya, yb, wa, wb, gates, tm, tn):
    T = ya.shape[0]
    nb = D_MODEL // tn
    return pl.pallas_call(
        _merge_kernel,
        grid=(T // tm, nb),
        in_specs=[pl.BlockSpec((tm, W_A), lambda i, j: (i, 0)),
                  pl.BlockSpec((tm, W_B), lambda i, j: (i, 0)),
                  pl.BlockSpec((W_A, tn), lambda i, j: (0, j)),
                  pl.BlockSpec((W_B, tn), lambda i, j: (0, j)),
                  pl.BlockSpec((tm, tn), lambda i, j: (i, j)),
                  pl.BlockSpec((tm, tn), lambda i, j: (i, nb + j))],
        out_specs=pl.BlockSpec((tm, tn), lambda i, j: (i, j)),
        out_shape=jax.ShapeDtypeStruct((T, D_MODEL), BF16),
        compiler_params=_params(("parallel", "arbitrary")),
        name="merge",
    )(ya, yb, wa, wb, gates, gates)


def _outproj_kernel(a_ref, w_ref, x_ref, o_ref):
    o_ref[...] = x_ref[...] + jnp.dot(a_ref[...], w_ref[...], preferred_element_type=F32)


def _outproj(merged, w, x, tm, tn):
    T = merged.shape[0]
    return pl.pallas_call(
        _outproj_kernel,
        grid=(T // tm, D_MODEL // tn),
        in_specs=[pl.BlockSpec((tm, D_MODEL), lambda i, j: (i, 0)),
                  pl.BlockSpec((D_MODEL, tn), lambda i, j: (0, j)),
                  pl.BlockSpec((tm, tn), lambda i, j: (i, j))],
        out_specs=pl.BlockSpec((tm, tn), lambda i, j: (i, j)),
        out_shape=jax.ShapeDtypeStruct((T, D_MODEL), F32),
        compiler_params=_params(("parallel", "arbitrary")),
        name="outproj",
    )(merged, w, x)


def _router_kernel(x_ref, g_ref, wr_ref, br_ref, xn_ref, r_ref):
    x = x_ref[...]
    ms = jnp.mean(x * x, axis=-1, keepdims=True)
    xn = x * lax.rsqrt(ms + EPS) * g_ref[...]
    xn_ref[...] = xn
    logits = jnp.dot(xn, wr_ref[...], precision=HIGHEST, preferred_element_type=F32) + br_ref[...]
    lane = lax.broadcasted_iota(jnp.int32, logits.shape, 1)
    cm = jnp.where(lane < N_GROUPS, logits, NEG)
    cmax = jnp.max(cm, axis=-1, keepdims=True)
    p_top = 1.0 / jnp.sum(jnp.exp(cm - cmax), axis=-1, keepdims=True)
    grp = jnp.min(jnp.where(cm == cmax, lane, LANES), axis=-1, keepdims=True)
    lo = N_GROUPS + EXP_PER_GROUP * grp
    fm = jnp.where((lane >= lo) & (lane < lo + EXP_PER_GROUP), logits, NEG)
    fmax = jnp.max(fm, axis=-1, keepdims=True)
    fe = jnp.exp(fm - fmax)
    pe = fe / jnp.sum(fe, axis=-1, keepdims=True)
    i1 = jnp.min(jnp.where(fm == fmax, lane, LANES), axis=-1, keepdims=True)
    p1 = jnp.sum(jnp.where(lane == i1, pe, 0.0), axis=-1, keepdims=True)
    fm2 = jnp.where(lane == i1, NEG, fm)
    f2max = jnp.max(fm2, axis=-1, keepdims=True)
    i2 = jnp.min(jnp.where(fm2 == f2max, lane, LANES), axis=-1, keepdims=True)
    p2 = jnp.sum(jnp.where(lane == i2, pe, 0.0), axis=-1, keepdims=True)
    g1 = p_top * (p1 / (p1 + p2))
    g2 = p_top * (p2 / (p1 + p2))
    e1 = (i1 - N_GROUPS).astype(F32)
    e2 = (i2 - N_GROUPS).astype(F32)
    r_ref[...] = jnp.where(lane == 0, e1, jnp.where(lane == 1, e2, jnp.where(lane == 2, g1,
                           jnp.where(lane == 3, g2, 0.0))))


def _router(x1, g, wr, br, tm):
    T = x1.shape[0]
    return pl.pallas_call(
        _router_kernel,
        grid=(T // tm,),
        in_specs=[pl.BlockSpec((tm, D_MODEL), lambda i: (i, 0)),
                  pl.BlockSpec((1, D_MODEL), lambda i: (0, 0)),
                  pl.BlockSpec((D_MODEL, LANES), lambda i: (0, 0)),
                  pl.BlockSpec((1, LANES), lambda i: (0, 0))],
        out_specs=[pl.BlockSpec((tm, D_MODEL), lambda i: (i, 0)),
                   pl.BlockSpec((tm, LANES), lambda i: (i, 0))],
        out_shape=[jax.ShapeDtypeStruct((T, D_MODEL), F32), jax.ShapeDtypeStruct((T, LANES), F32)],
        compiler_params=_params(("parallel",)),
        name="router",
    )(x1, g.reshape(1, D_MODEL), wr, br)


def _row_copy(src_hbm, row, dst, r, sem):
    return pltpu.make_async_copy(src_hbm.at[pl.ds(row, 1)], dst.at[pl.ds(r, 1)], sem)


def _gather_kernel(idx_ref, src_hbm, o_ref, sem, *, rows):
    base = pl.program_id(0) * rows

    def start(r, carry):
        _row_copy(src_hbm, idx_ref[base + r], o_ref, r, sem).start()
        return carry

    def wait(r, carry):
        _row_copy(src_hbm, 0, o_ref, r, sem).wait()
        return carry

    lax.fori_loop(0, rows, start, 0)
    lax.fori_loop(0, rows, wait, 0)


def _gather_rows(src, idx, rows):
    n = idx.shape[0]
    D = src.shape[1]
    return pl.pallas_call(
        functools.partial(_gather_kernel, rows=rows),
        grid_spec=pltpu.PrefetchScalarGridSpec(
            num_scalar_prefetch=1,
            grid=(n // rows,),
            in_specs=[pl.BlockSpec(memory_space=pl.ANY)],
            out_specs=pl.BlockSpec((rows, D), lambda i, idx: (i, 0)),
            scratch_shapes=[pltpu.SemaphoreType.DMA(())],
        ),
        out_shape=jax.ShapeDtypeStruct((n, D), src.dtype),
        compiler_params=_params(("arbitrary",)),
        name="moe_gather",
    )(idx, src)


def _moe_kernel(be_ref, nused_ref, xs_ref, w1_ref, w3_ref, w2_ref, o_ref):
    b = pl.program_id(0)
    f = pl.program_id(1)

    @pl.when(b < nused_ref[0])
    def _():
        x = xs_ref[...].astype(BF16)
        h1 = jnp.dot(x, w1_ref[...], preferred_element_type=F32)
        h3 = jnp.dot(x, w3_ref[...], preferred_element_type=F32)
        hid = (h1 * jax.nn.sigmoid(h1)) * h3
        out = jnp.dot(hid.astype(BF16), w2_ref[...], preferred_element_type=F32)

        @pl.when(f == 0)
        def _():
            o_ref[...] = out

        @pl.when(f > 0)
        def _():
            o_ref[...] += out

    @pl.when((b >= nused_ref[0]) & (f == 0))
    def _():
        o_ref[...] = jnp.zeros(o_ref.shape, o_ref.dtype)


def _moe_mlp(xs, w1, w3, w2, blk_exp, nused):
    ns = xs.shape[0]
    nb = ns // MOE_ROWS
    nf = D_FF_EXP // FF_CHUNK
    return pl.pallas_call(
        _moe_kernel,
        grid_spec=pltpu.PrefetchScalarGridSpec(
            num_scalar_prefetch=2,
            grid=(nb, nf),
            in_specs=[
                pl.BlockSpec((MOE_ROWS, D_MODEL), lambda b, f, be, nu: (b, 0)),
                pl.BlockSpec((None, D_MODEL, FF_CHUNK), lambda b, f, be, nu: (be[b], 0, f)),
                pl.BlockSpec((None, D_MODEL, FF_CHUNK), lambda b, f, be, nu: (be[b], 0, f)),
                pl.BlockSpec((None, FF_CHUNK, D_MODEL), lambda b, f, be, nu: (be[b], f, 0)),
            ],
            out_specs=pl.BlockSpec((MOE_ROWS, D_MODEL), lambda b, f, be, nu: (b, 0)),
        ),
        out_shape=jax.ShapeDtypeStruct((ns, D_MODEL), F32),
        compiler_params=_params(("arbitrary", "arbitrary")),
        name="moe_mlp",
    )(blk_exp, nused, xs, w1, w3, w2)


def _combine_kernel(pos_ref, x_ref, r_ref, o_hbm, y_ref, buf, sem, *, rows):
    base = pl.program_id(0) * rows

    def start(r, carry):
        for k in range(2):
            _row_copy(o_hbm, pos_ref[2 * (base + r) + k], buf.at[k], r, sem).start()
        return carry

    def wait(r, carry):
        for k in range(2):
            _row_copy(o_hbm, 0, buf.at[k], r, sem).wait()
        return carry

    lax.fori_loop(0, rows, start, 0)
    lax.fori_loop(0, rows, wait, 0)
    r = r_ref[...]
    y_ref[...] = x_ref[...] + (buf[0] * r[:, 2:3] + buf[1] * r[:, 3:4])


def _combine(x1, route, o, pos, rows):
    T, D = x1.shape
    return pl.pallas_call(
        functools.partial(_combine_kernel, rows=rows),
        grid_spec=pltpu.PrefetchScalarGridSpec(
            num_scalar_prefetch=1,
            grid=(T // rows,),
            in_specs=[pl.BlockSpec((rows, D), lambda i, pos: (i, 0)),
                      pl.BlockSpec((rows, LANES), lambda i, pos: (i, 0)),
                      pl.BlockSpec(memory_space=pl.ANY)],
            out_specs=pl.BlockSpec((rows, D), lambda i, pos: (i, 0)),
            scratch_shapes=[pltpu.VMEM((2, rows, D), F32), pltpu.SemaphoreType.DMA(())],
        ),
        out_shape=jax.ShapeDtypeStruct((T, D), F32),
        compiler_params=_params(("arbitrary",)),
        name="moe_combine",
    )(pos, x1, route, o)


def _dispatch_plan(route, T):
    i32 = jnp.int32
    tk = 2 * T
    nb = tk // MOE_ROWS + N_EXP
    experts = jnp.arange(N_EXP, dtype=i32)
    e_flat = route[:, 0:2].astype(i32).reshape(-1)
    order = jnp.argsort(e_flat).astype(i32)
    rank = jnp.argsort(order).astype(i32)
    counts = jnp.sum(e_flat[:, None] == experts[None, :], axis=0, dtype=i32)
    pcounts = (counts + MOE_ROWS - 1) // MOE_ROWS * MOE_ROWS
    pends = jnp.cumsum(pcounts)
    pstarts = pends - pcounts
    ustarts = jnp.cumsum(counts) - counts
    pos = pstarts[e_flat] + rank - ustarts[e_flat]
    slot = jnp.arange(nb * MOE_ROWS, dtype=i32)
    slot_exp = jnp.minimum(jnp.sum(pends[None, :] <= slot[:, None], axis=1, dtype=i32), N_EXP - 1)
    within = slot - pstarts[slot_exp]
    src = jnp.clip(ustarts[slot_exp] + within, 0, tk - 1)
    slot_tok = jnp.where(within < counts[slot_exp], order[src] // 2, 0).astype(i32)
    blk_exp = slot_exp[::MOE_ROWS]
    nused = (pends[-1:] // MOE_ROWS).astype(i32)
    return slot_tok, blk_exp, nused, pos.astype(i32)


def _tile(n, pref):
    return pref if n % pref == 0 else n


def _trunk(x, p):
    B, S, D = x.shape
    T = B * S
    xt = x.reshape(T, D)
    tm = _tile(T, 1024)
    tr = _tile(T, 256)

    xn = _rmsnorm(xt, p["norm_mix_g"], tr)
    qk = _proj(xn, p["w_main"], 2 * W_A, BF16, tm, 512, col_off=0, vec=p["qk_gain"], kind="headnorm",
               name="proj_qk")
    vqkv = _proj(xn, p["w_main"], W_A + 2 * QK_B + W_B, BF16, tm, 512, col_off=2 * W_A, name="proj_vqkv")
    om = _proj(xn, p["w_main"], W_B, F32, tm, 512, col_off=3 * W_A + 2 * QK_B + W_B, name="proj_o")
    gates = _proj(xn, p["w_gate"], 2 * D_MODEL, F32, tm, 512, name="proj_gates")
    gif = _proj(xn, p["w_gif"], LANES, F32, tm, LANES, vec=p["b_gif"], kind="bias", name="proj_gif")

    ya = _diff_attention(qk, vqkv, p["slopes"], p["lam"], p["subln_scaled"], B, S,
                         _tile(S, 512), _tile(S, 2048))

    g4 = gif[:, :4 * N_HEADS_B].reshape(B, S, 4, N_HEADS_B)
    gcol = g4.transpose(0, 3, 1, 2)
    grow = g4.transpose(0, 3, 2, 1)
    hf, hb = _mlstm(vqkv, gcol, grow, B, S)
    yb = _mlstm_out(hf, hb, om, p["mlstm_norm_g"], tr)

    merged = _merge(ya, yb, p["w_branch_a"], p["w_branch_b"], gates, tm, 512)
    x1 = _outproj(merged, p["w_out"], xt, tm, 512)

    xn2, route = _router(x1, p["norm_ffn_g"], p["w_router"], p["b_router"], tr)
    slot_tok, blk_exp, nused, pos = _dispatch_plan(route, T)
    xs = _gather_rows(xn2, slot_tok, MOE_ROWS)
    o = _moe_mlp(xs, p["w1"], p["w3"], p["w2"], blk_exp, nused)
    y = _combine(x1, route, o, pos, _tile(T, 128))
    return y.reshape(B, S, D)


def _prepare(norm_mix_g, w_in, b_mgate, attn_q_norm_g, attn_k_norm_g, lambda_q1, lambda_k1, lambda_q2,
             lambda_k2, attn_subln_g, mlstm_norm_g, w_branch_a, w_branch_b, w_out, norm_ffn_g,
             w_router_group, b_router_group, w_router_expert, b_router_expert, w_exp_gate, w_exp_up,
             w_exp_down):
    l = 0
    n_main = 3 * W_A + 2 * QK_B + 2 * W_B
    n_if = 4 * N_HEADS_B
    w = w_in[l]
    q_gain = jnp.tile((attn_q_norm_g[l].astype(F32) * (HEAD_DIM_A ** -0.5 * LOG2E)).reshape(-1), N_HEADS_A)
    k_gain = jnp.tile(attn_k_norm_g[l].astype(F32).reshape(-1), N_HEADS_A)
    lam = (jnp.exp(jnp.sum(lambda_q1[l].astype(F32) * lambda_k1[l].astype(F32)))
           - jnp.exp(jnp.sum(lambda_q2[l].astype(F32) * lambda_k2[l].astype(F32))) + LAM_INIT)
    slopes = LOG2E * 2.0 ** (-8.0 * (jnp.arange(N_HEADS_A, dtype=F32) + 1.0) / N_HEADS_A)
    w_router = jnp.zeros((D_MODEL, LANES), F32)
    w_router = w_router.at[:, :N_GROUPS].set(w_router_group[l])
    w_router = w_router.at[:, N_GROUPS:N_GROUPS + N_EXP].set(w_router_expert[l])
    b_router = jnp.zeros((1, LANES), F32)
    b_router = b_router.at[0, :N_GROUPS].set(b_router_group[l].astype(F32))
    b_router = b_router.at[0, N_GROUPS:N_GROUPS + N_EXP].set(b_router_expert[l].astype(F32).reshape(-1))
    return {
        "norm_mix_g": norm_mix_g[l],
        "w_main": w[:, :n_main].astype(BF16),
        "w_gate": w[:, n_main + n_if:].astype(BF16),
        "w_gif": jnp.pad(w[:, n_main:n_main + n_if], ((0, 0), (0, LANES - n_if))).astype(BF16),
        "b_gif": jnp.pad(b_mgate[l].astype(F32), (0, LANES - n_if)),
        "qk_gain": jnp.concatenate([q_gain, k_gain]),
        "lam": lam.reshape(1),
        "slopes": slopes,
        "subln_scaled": attn_subln_g[l].astype(F32) * (1.0 - LAM_INIT),
        "mlstm_norm_g": mlstm_norm_g[l],
        "w_branch_a": w_branch_a[l].astype(BF16),
        "w_branch_b": w_branch_b[l].astype(BF16),
        "w_out": w_out[l].astype(BF16),
        "norm_ffn_g": norm_ffn_g[l],
        "w_router": w_router,
        "b_router": b_router,
        "w1": w_exp_gate[l].astype(BF16),
        "w3": w_exp_up[l].astype(BF16),
        "w2": w_exp_down[l].astype(BF16),
    }


def kernel(x_prompt, x_sample, norm_mix_g, w_in, b_mgate, attn_q_norm_g, attn_k_norm_g, lambda_q1, lambda_k1, lambda_q2, lambda_k2, attn_subln_g, mlstm_norm_g, w_branch_a, w_branch_b, w_out, norm_ffn_g, w_router_group, b_router_group, w_router_expert, b_router_expert, w_exp_gate, w_exp_up, w_exp_down):
    p = _prepare(norm_mix_g, w_in, b_mgate, attn_q_norm_g, attn_k_norm_g, lambda_q1, lambda_k1, lambda_q2,
                 lambda_k2, attn_subln_g, mlstm_norm_g, w_branch_a, w_branch_b, w_out, norm_ffn_g,
                 w_router_group, b_router_group, w_router_expert, b_router_expert, w_exp_gate, w_exp_up,
                 w_exp_down)
    return (_trunk(x_prompt, p), _trunk(x_sample, p))
```

```python
import functools

import jax
import jax.numpy as jnp
from jax import lax
from jax.experimental import pallas as pl
from jax.experimental.pallas import tpu as pltpu

D_MODEL = 4096
HEAD_DIM_A = 128
N_HEADS_A = 8
W_A = 2048
N_HEADS_B = 4
DV_B = 512
DK_B = 256
W_B = 2048
QK_B = 1024
CHUNK = 128
N_GROUPS = 4
EXP_PER_GROUP = 8
N_EXP = 32
D_FF_EXP = 1024
EPS = 1e-6
NEG = -1e30
LAM_INIT = 0.8 - 0.6 * 1.0

LANES = 128
MOE_ROWS = 256
PROJ_TM, PROJ_TN = 1024, 1024
MERGE_TN = 512
ROW_TM = 256
ATTN_TQ, ATTN_TK = 512, 2048
VMEM_LIMIT = 56 * 1024 * 1024
ATTN_ROWS = 16
LOG2E = 1.4426950408889634

F32 = jnp.float32
BF16 = jnp.bfloat16
HIGHEST = lax.Precision.HIGHEST


def _params(sem, vmem=VMEM_LIMIT):
    return pltpu.CompilerParams(dimension_semantics=sem, vmem_limit_bytes=vmem)


def _rmsnorm_kernel(x_ref, g_ref, o_ref):
    x = x_ref[...]
    ms = jnp.mean(x * x, axis=-1, keepdims=True)
    o_ref[...] = (x * lax.rsqrt(ms + EPS) * g_ref[...]).astype(o_ref.dtype)


def _rmsnorm(x, g, tm):
    T, D = x.shape
    return pl.pallas_call(
        _rmsnorm_kernel,
        grid=(T // tm,),
        in_specs=[pl.BlockSpec((tm, D), lambda i: (i, 0)), pl.BlockSpec((1, D), lambda i: (0, 0))],
        out_specs=pl.BlockSpec((tm, D), lambda i: (i, 0)),
        out_shape=jax.ShapeDtypeStruct((T, D), BF16),
        compiler_params=_params(("parallel",)),
        name="rmsnorm",
    )(x, g.reshape(1, D))


def _proj_kernel(a_ref, w_ref, o_ref):
    o_ref[...] = jnp.dot(a_ref[...], w_ref[...], preferred_element_type=F32).astype(o_ref.dtype)


def _proj_bias_kernel(a_ref, w_ref, b_ref, o_ref):
    acc = jnp.dot(a_ref[...], w_ref[...], preferred_element_type=F32)
    o_ref[...] = (acc + b_ref[...]).astype(o_ref.dtype)


def _proj_headnorm_kernel(a_ref, w_ref, g_ref, o_ref):
    acc = jnp.dot(a_ref[...], w_ref[...], preferred_element_type=F32)
    for j in range(acc.shape[1] // HEAD_DIM_A):
        sl = slice(j * HEAD_DIM_A, (j + 1) * HEAD_DIM_A)
        blk = acc[:, sl]
        ms = jnp.mean(blk * blk, axis=-1, keepdims=True)
        o_ref[:, sl] = (blk * lax.rsqrt(ms + EPS) * g_ref[:, sl]).astype(o_ref.dtype)


def _proj(a, w, n_out, out_dtype, tm, tn, *, col_off=0, vec=None, kind="plain", name="proj"):
    T, K = a.shape
    assert n_out % tn == 0 and col_off % tn == 0 and T % tm == 0
    off = col_off // tn
    in_specs = [pl.BlockSpec((tm, K), lambda i, j: (i, 0)),
                pl.BlockSpec((K, tn), lambda i, j: (0, j + off))]
    args = [a, w]
    if kind == "plain":
        body = _proj_kernel
    else:
        body = _proj_bias_kernel if kind == "bias" else _proj_headnorm_kernel
        in_specs.append(pl.BlockSpec((1, tn), lambda i, j: (0, j)))
        args.append(vec.reshape(1, n_out).astype(F32))
    return pl.pallas_call(
        body,
        grid=(T // tm, n_out // tn),
        in_specs=in_specs,
        out_specs=pl.BlockSpec((tm, tn), lambda i, j: (i, j)),
        out_shape=jax.ShapeDtypeStruct((T, n_out), out_dtype),
        compiler_params=_params(("parallel", "arbitrary")),
        name=name,
    )(*args)


def _attn_kernel(slopes_ref, lam_ref, q_ref, k_ref, v_ref, g_ref, o_ref,
                 s_sc, p_sc, pm_sc, ps_sc, e_sc, m_sc, l_sc, acc_sc, *, tq, tk):
    h = pl.program_id(1)
    qi = pl.program_id(2)
    ki = pl.program_id(3)
    rows_per = ATTN_ROWS
    n_lane_tiles = tk // LANES

    @pl.when(ki == 0)
    def _():
        m_sc[...] = jnp.full(m_sc.shape, NEG, F32)
        l_sc[...] = jnp.zeros(l_sc.shape, F32)
        acc_sc[...] = jnp.zeros(acc_sc.shape, F32)

    slope = slopes_ref[h]
    off = qi * tq - ki * tk
    left = off >= tk - 1
    right = off <= 1 - tq
    separable = left | right
    col = lax.broadcasted_iota(jnp.int32, (1, tk), 1)
    row = lax.broadcasted_iota(jnp.int32, (rows_per, 1), 0)
    all_rows = lax.broadcasted_iota(jnp.int32, (tq, 1), 0)
    col_bias = slope * jnp.where(left, tk - 1 - col, col).astype(F32)
    row_term = jnp.where(left, all_rows + (off - (tk - 1)), -off - all_rows).astype(F32)
    shift = jnp.where(separable, -slope * row_term, 0.0)

    def lane_tile(j):
        return slice(j * LANES, (j + 1) * LANES)

    def row_group(i):
        return slice(i * rows_per, (i + 1) * rows_per)

    def pass_a(c, is_separable):
        for i in range(tq // rows_per):
            rows = row_group(i)
            if is_separable:
                t = s_sc[c, rows, :] - col_bias
            else:
                d = (row + (i * rows_per + off)) - col
                t = s_sc[c, rows, :] - slope * jnp.abs(d.astype(F32))
                s_sc[c, rows, :] = t
            pm = t[:, lane_tile(0)]
            for j in range(1, n_lane_tiles):
                pm = jnp.maximum(pm, t[:, lane_tile(j)])
            pm_sc[c, rows, :] = pm

    def pass_b(c, is_separable):
        for i in range(tq // rows_per):
            rows = row_group(i)
            e = e_sc[c, rows, :]
            ps = None
            for j in range(n_lane_tiles):
                t = s_sc[c, rows, lane_tile(j)]
                if is_separable:
                    t = t - col_bias[:, lane_tile(j)]
                p = jnp.exp2(t - e)
                p_sc[c, rows, lane_tile(j)] = p.astype(BF16)
                ps = p if ps is None else ps + p
            ps_sc[c, rows, :] = ps

    def tile(is_separable):
        v = v_ref[...]
        for c in range(2):
            sl = slice(c * HEAD_DIM_A, (c + 1) * HEAD_DIM_A)
            s_sc[c] = lax.dot_general(q_ref[:, sl], k_ref[:, sl], (((1,), (1,)), ((), ())),
                                      preferred_element_type=F32)
        for c in range(2):
            pass_a(c, is_separable)
            m_prev = m_sc[c]
            m_new = jnp.maximum(m_prev, jnp.max(pm_sc[c], axis=-1, keepdims=True) + shift)
            alpha = jnp.exp2(m_prev - m_new)
            m_sc[c] = m_new
            e_sc[c] = jnp.broadcast_to(m_new - shift, (tq, LANES))
            pass_b(c, is_separable)
            l_sc[c] = alpha * l_sc[c] + jnp.sum(ps_sc[c], axis=-1, keepdims=True)
            acc_sc[c] = alpha * acc_sc[c] + jnp.dot(p_sc[c], v, preferred_element_type=F32)

    @pl.when(separable)
    def _():
        tile(True)

    @pl.when(jnp.logical_not(separable))
    def _():
        tile(False)

    @pl.when(ki == pl.num_programs(3) - 1)
    def _():
        o = acc_sc[0] / l_sc[0] - lam_ref[0] * (acc_sc[1] / l_sc[1])
        ms = jnp.mean(o * o, axis=-1, keepdims=True)
        o_ref[...] = (o * lax.rsqrt(ms + EPS) * g_ref[...]).astype(o_ref.dtype)


def _diff_attention(qk, vqkv, slopes, lam, subln_scaled, B, S, tq, tk):
    T = B * S
    nq, nk = S // tq, S // tk
    hw = 2 * HEAD_DIM_A
    return pl.pallas_call(
        functools.partial(_attn_kernel, tq=tq, tk=tk),
        grid=(B, N_HEADS_A, nq, nk),
        in_specs=[
            pl.BlockSpec(memory_space=pltpu.SMEM),
            pl.BlockSpec(memory_space=pltpu.SMEM),
            pl.BlockSpec((tq, hw), lambda b, h, i, j: (b * nq + i, h)),
            pl.BlockSpec((tk, hw), lambda b, h, i, j: (b * nk + j, N_HEADS_A + h)),
            pl.BlockSpec((tk, hw), lambda b, h, i, j: (b * nk + j, h)),
            pl.BlockSpec((1, hw), lambda b, h, i, j: (0, 0)),
        ],
        out_specs=pl.BlockSpec((tq, hw), lambda b, h, i, j: (b * nq + i, h)),
        out_shape=jax.ShapeDtypeStruct((T, W_A), BF16),
        scratch_shapes=[pltpu.VMEM((2, tq, tk), F32), pltpu.VMEM((2, tq, tk), BF16),
                        pltpu.VMEM((2, tq, LANES), F32), pltpu.VMEM((2, tq, LANES), F32),
                        pltpu.VMEM((2, tq, LANES), F32),
                        pltpu.VMEM((2, tq, 1), F32), pltpu.VMEM((2, tq, 1), F32),
                        pltpu.VMEM((2, tq, hw), F32)],
        compiler_params=_params(("parallel", "parallel", "parallel", "arbitrary")),
        name="diff_attention",
    )(slopes, lam, qk, qk, vqkv, subln_scaled.reshape(1, hw))


def _log_sigmoid(x):
    return jnp.minimum(x, 0.0) - jnp.log1p(jnp.exp(-jnp.abs(x)))


def _mlstm_chunk(q, k, v, gcol, grow, c_ref, n_ref, m_ref, reverse):
    L = q.shape[0]
    ii, fi = (2, 3) if reverse else (0, 1)
    r = lax.broadcasted_iota(jnp.int32, (L, L), 0)
    c = lax.broadcasted_iota(jnp.int32, (L, L), 1)
    lower = c <= r
    upper = c >= r
    tri_lo = jnp.where(lower, 1.0, 0.0).astype(F32)
    tri_up = jnp.where(upper, 1.0, 0.0).astype(F32)
    ic_row = grow[ii:ii + 1, :]
    ic_col = gcol[:, ii:ii + 1]
    lf_row = _log_sigmoid(grow[fi:fi + 1, :])
    lf_col = _log_sigmoid(gcol[:, fi:fi + 1])
    if reverse:
        b_row = jnp.dot(lf_row, tri_lo, precision=HIGHEST, preferred_element_type=F32)
        b_col = jnp.dot(tri_up, lf_col, precision=HIGHEST, preferred_element_type=F32)
        mask = upper
        b_last = b_row[:, 0:1]
    else:
        b_row = jnp.dot(lf_row, tri_up, precision=HIGHEST, preferred_element_type=F32)
        b_col = jnp.dot(tri_lo, lf_col, precision=HIGHEST, preferred_element_type=F32)
        mask = lower
        b_last = b_row[:, L - 1:L]
    m = m_ref[...]
    log_d = jnp.where(mask, b_col - b_row + ic_row, NEG)
    inter = b_col + m
    m_row = jnp.maximum(inter, jnp.max(log_d, axis=-1, keepdims=True))
    d_m = jnp.exp(log_d - m_row)
    sc = jnp.exp(inter - m_row)
    qs = q * (DK_B ** -0.5)
    s = lax.dot_general(qs, k, (((1,), (1,)), ((), ())), preferred_element_type=F32) * d_m
    c_state = c_ref[...]
    n_state = n_ref[...]
    num = (jnp.dot(s.astype(BF16), v, preferred_element_type=F32)
           + sc * jnp.dot(qs, c_state.astype(BF16), preferred_element_type=F32))
    den = (jnp.sum(s, axis=-1, keepdims=True)
           + sc * jnp.sum(qs.astype(F32) * n_state, axis=-1, keepdims=True))
    h = num / jnp.maximum(jnp.abs(den), jnp.exp(-m_row))
    w_col = b_last - b_col + ic_col
    w_row = b_last - b_row + ic_row
    m_new = jnp.maximum(b_last + m, jnp.max(w_row, axis=-1, keepdims=True))
    decay = jnp.exp(b_last + m - m_new)
    ws_col = jnp.exp(w_col - m_new)
    wv = (ws_col * v.astype(F32)).astype(BF16)
    c_ref[...] = decay * c_state + lax.dot_general(k, wv, (((0,), (0,)), ((), ())),
                                                   preferred_element_type=F32)
    n_ref[...] = decay * n_state + jnp.sum(ws_col * k.astype(F32), axis=0, keepdims=True)
    m_ref[...] = m_new
    return h


def _mlstm_kernel(qf_ref, kf_ref, vf_ref, gcf_ref, grf_ref, qb_ref, kb_ref, vb_ref, gcb_ref, grb_ref,
                  hf_ref, hb_ref, c_sc, n_sc, m_sc):
    @pl.when(pl.program_id(1) == 0)
    def _():
        c_sc[...] = jnp.zeros(c_sc.shape, F32)
        n_sc[...] = jnp.zeros(n_sc.shape, F32)
        m_sc[...] = jnp.full(m_sc.shape, NEG, F32)

    for h in range(N_HEADS_B):
        qk = slice(h * DK_B, (h + 1) * DK_B)
        vv = slice(h * DV_B, (h + 1) * DV_B)
        hf_ref[:, vv] = _mlstm_chunk(qf_ref[:, qk], kf_ref[:, qk], vf_ref[:, vv], gcf_ref[h], grf_ref[h],
                                     c_sc.at[0, h], n_sc.at[0, h], m_sc.at[0, h], reverse=False)
        hb_ref[:, vv] = _mlstm_chunk(qb_ref[:, qk], kb_ref[:, qk], vb_ref[:, vv], gcb_ref[h], grb_ref[h],
                                     c_sc.at[1, h], n_sc.at[1, h], m_sc.at[1, h], reverse=True)


def _mlstm(vqkv, gcol, grow, B, S):
    T = B * S
    nc = S // CHUNK
    q_off = W_A // QK_B
    k_off = (W_A + QK_B) // QK_B
    v_off = (W_A + 2 * QK_B) // W_B

    def fwd(b, c):
        return b * nc + c

    def bwd(b, c):
        return b * nc + nc - 1 - c

    def specs(pos):
        return [
            pl.BlockSpec((CHUNK, QK_B), lambda b, c: (pos(b, c), q_off)),
            pl.BlockSpec((CHUNK, QK_B), lambda b, c: (pos(b, c), k_off)),
            pl.BlockSpec((CHUNK, W_B), lambda b, c: (pos(b, c), v_off)),
            pl.BlockSpec((None, N_HEADS_B, CHUNK, 4), lambda b, c: (b, 0, pos(b, c) - b * nc, 0)),
            pl.BlockSpec((None, N_HEADS_B, 4, CHUNK), lambda b, c: (b, 0, 0, pos(b, c) - b * nc)),
        ]

    out_sds = jax.ShapeDtypeStruct((T, W_B), F32)
    return pl.pallas_call(
        _mlstm_kernel,
        grid=(B, nc),
        in_specs=specs(fwd) + specs(bwd),
        out_specs=[pl.BlockSpec((CHUNK, W_B), lambda b, c: (fwd(b, c), 0)),
                   pl.BlockSpec((CHUNK, W_B), lambda b, c: (bwd(b, c), 0))],
        out_shape=[out_sds, out_sds],
        scratch_shapes=[pltpu.VMEM((2, N_HEADS_B, DK_B, DV_B), F32), pltpu.VMEM((2, N_HEADS_B, 1, DK_B), F32),
                        pltpu.VMEM((2, N_HEADS_B, 1, 1), F32)],
        compiler_params=_params(("parallel", "arbitrary")),
        name="mlstm",
    )(vqkv, vqkv, vqkv, gcol, grow, vqkv, vqkv, vqkv, gcol, grow)


def _mlstm_out_kernel(hf_ref, hb_ref, o_ref, g_ref, y_ref):
    for j in range(N_HEADS_B):
        sl = slice(j * DV_B, (j + 1) * DV_B)
        h = hf_ref[:, sl] + hb_ref[:, sl]
        ms = jnp.mean(h * h, axis=-1, keepdims=True)
        hn = h * lax.rsqrt(ms + EPS) * g_ref[...]
        y_ref[:, sl] = (jax.nn.sigmoid(o_ref[:, sl]) * hn).astype(y_ref.dtype)


def _mlstm_out(hf, hb, om, norm_g, tm):
    T = hf.shape[0]
    row = pl.BlockSpec((tm, W_B), lambda i: (i, 0))
    return pl.pallas_call(
        _mlstm_out_kernel,
        grid=(T // tm,),
        in_specs=[row, row, row, pl.BlockSpec((1, DV_B), lambda i: (0, 0))],
        out_specs=row,
        out_shape=jax.ShapeDtypeStruct((T, W_B), BF16),
        compiler_params=_params(("parallel",)),
        name="mlstm_out",
    )(hf, hb, om, norm_g.reshape(1, DV_B))


def _merge_kernel(ya_ref, yb_ref, wa_ref, wb_ref, ga_ref, gb_ref, o_ref):
    pa = jnp.dot(ya_ref[...], wa_ref[...], preferred_element_type=F32)
    pb = jnp.dot(yb_ref[...], wb_ref[...], preferred_element_type=F32)
    o_ref[...] = (jax.nn.sigmoid(ga_ref[...]) * pa + jax.nn.sigmoid(gb_ref[...]) * pb).astype(o_ref.dtype)


def _merge(ya, yb, wa, wb, gates, tm, tn):
    T = ya.shape[0]
    nb = D_MODEL // tn
    return pl.pallas_call(
        _merge_kernel,
        grid=(T // tm, nb),
        in_specs=[pl.BlockSpec((tm, W_A), lambda i, j: (i, 0)),
                  pl.BlockSpec((tm, W_B), lambda i, j: (i, 0)),
                  pl.BlockSpec((W_A, tn), lambda i, j: (0, j)),
                  pl.BlockSpec((W_B, tn), lambda i, j: (0, j)),
                  pl.BlockSpec((tm, tn), lambda i, j: (i, j)),
                  pl.BlockSpec((tm, tn), lambda i, j: (i, nb + j))],
        out_specs=pl.BlockSpec((tm, tn), lambda i, j: (i, j)),
        out_shape=jax.ShapeDtypeStruct((T, D_MODEL), BF16),
        compiler_params=_params(("parallel", "arbitrary")),
        name="merge",
    )(ya, yb, wa, wb, gates, gates)


def _outproj_kernel(a_ref, w_ref, x_ref, o_ref):
    o_ref[...] = x_ref[...] + jnp.dot(a_ref[...], w_ref[...], preferred_element_type=F32)


def _outproj(merged, w, x, tm, tn):
    T = merged.shape[0]
    return pl.pallas_call(
        _outproj_kernel,
        grid=(T // tm, D_MODEL // tn),
        in_specs=[pl.BlockSpec((tm, D_MODEL), lambda i, j: (i, 0)),
                  pl.BlockSpec((D_MODEL, tn), lambda i, j: (0, j)),
                  pl.BlockSpec((tm, tn), lambda i, j: (i, j))],
        out_specs=pl.BlockSpec((tm, tn), lambda i, j: (i, j)),
        out_shape=jax.ShapeDtypeStruct((T, D_MODEL), F32),
        compiler_params=_params(("parallel", "arbitrary")),
        name="outproj",
    )(merged, w, x)


def _router_kernel(x_ref, g_ref, wr_ref, br_ref, xn_ref, r_ref):
    x = x_ref[...]
    ms = jnp.mean(x * x, axis=-1, keepdims=True)
    xn = x * lax.rsqrt(ms + EPS) * g_ref[...]
    xn_ref[...] = xn
    logits = jnp.dot(xn, wr_ref[...], precision=HIGHEST, preferred_element_type=F32) + br_ref[...]
    lane = lax.broadcasted_iota(jnp.int32, logits.shape, 1)
    cm = jnp.where(lane < N_GROUPS, logits, NEG)
    cmax = jnp.max(cm, axis=-1, keepdims=True)
    p_top = 1.0 / jnp.sum(jnp.exp(cm - cmax), axis=-1, keepdims=True)
    grp = jnp.min(jnp.where(cm == cmax, lane, LANES), axis=-1, keepdims=True)
    lo = N_GROUPS + EXP_PER_GROUP * grp
    fm = jnp.where((lane >= lo) & (lane < lo + EXP_PER_GROUP), logits, NEG)
    fmax = jnp.max(fm, axis=-1, keepdims=True)
    fe = jnp.exp(fm - fmax)
    pe = fe / jnp.sum(fe, axis=-1, keepdims=True)
    i1 = jnp.min(jnp.where(fm == fmax, lane, LANES), axis=-1, keepdims=True)
    p1 = jnp.sum(jnp.where(lane == i1, pe, 0.0), axis=-1, keepdims=True)
    fm2 = jnp.where(lane == i1, NEG, fm)
    f2max = jnp.max(fm2, axis=-1, keepdims=True)
    i2 = jnp.min(jnp.where(fm2 == f2max, lane, LANES), axis=-1, keepdims=True)
    p2 = jnp.sum(jnp.where(lane == i2, pe, 0.0), axis=-1, keepdims=True)
    g1 = p_top * (p1 / (p1 + p2))
    g2 = p_top * (p2 / (p1 + p2))
    e1 = (i1 - N_GROUPS).astype(F32)
    e2 = (i2 - N_GROUPS).astype(F32)
    r_ref[...] = jnp.where(lane == 0, e1, jnp.where(lane == 1, e2, jnp.where(lane == 2, g1,
                           jnp.where(lane == 3, g2, 0.0))))


def _router(x1, g, wr, br, tm):
    T = x1.shape[0]
    return pl.pallas_call(
        _router_kernel,
        grid=(T // tm,),
        in_specs=[pl.BlockSpec((tm, D_MODEL), lambda i: (i, 0)),
                  pl.BlockSpec((1, D_MODEL), lambda i: (0, 0)),
                  pl.BlockSpec((D_MODEL, LANES), lambda i: (0, 0)),
                  pl.BlockSpec((1, LANES), lambda i: (0, 0))],
        out_specs=[pl.BlockSpec((tm, D_MODEL), lambda i: (i, 0)),
                   pl.BlockSpec((tm, LANES), lambda i: (i, 0))],
        out_shape=[jax.ShapeDtypeStruct((T, D_MODEL), F32), jax.ShapeDtypeStruct((T, LANES), F32)],
        compiler_params=_params(("parallel",)),
        name="router",
    )(x1, g.reshape(1, D_MODEL), wr, br)


def _row_copy(src_hbm, row, dst, r, sem):
    return pltpu.make_async_copy(src_hbm.at[pl.ds(row, 1)], dst.at[pl.ds(r, 1)], sem)


def _moe_kernel(tok_ref, be_ref, nused_ref, x_hbm, w1_ref, w3_ref, w2_ref, o_ref, xbuf, sems):
    b = pl.program_id(0)
    nused = nused_ref[0]
    cur = lax.rem(b, 2)

    def gather(blk, buf, fn):
        def body(r, carry):
            fn(_row_copy(x_hbm, tok_ref[blk * MOE_ROWS + r], xbuf.at[buf], r, sems.at[buf]))
            return carry

        lax.fori_loop(0, MOE_ROWS, body, 0, unroll=8)

    @pl.when(b == 0)
    def _():
        gather(0, 0, lambda cp: cp.start())

    @pl.when(b + 1 < nused)
    def _():
        gather(b + 1, 1 - cur, lambda cp: cp.start())

    @pl.when(b < nused)
    def _():
        gather(b, cur, lambda cp: cp.wait())
        x = xbuf[cur].astype(BF16)
        h1 = jnp.dot(x, w1_ref[...], preferred_element_type=F32)
        h3 = jnp.dot(x, w3_ref[...], preferred_element_type=F32)
        hid = (h1 * jax.nn.sigmoid(h1)) * h3
        o_ref[...] = jnp.dot(hid.astype(BF16), w2_ref[...], preferred_element_type=F32)

    @pl.when(b >= nused)
    def _():
        o_ref[...] = jnp.zeros(o_ref.shape, o_ref.dtype)


def _moe_mlp(xn, slot_tok, w1, w3, w2, blk_exp, nused):
    ns = slot_tok.shape[0]
    nb = ns // MOE_ROWS
    once = pl.Buffered(1)
    return pl.pallas_call(
        _moe_kernel,
        grid_spec=pltpu.PrefetchScalarGridSpec(
            num_scalar_prefetch=3,
            grid=(nb,),
            in_specs=[
                pl.BlockSpec(memory_space=pl.ANY),
                pl.BlockSpec((None, D_MODEL, D_FF_EXP), lambda b, tok, be, nu: (be[b], 0, 0), pipeline_mode=once),
                pl.BlockSpec((None, D_MODEL, D_FF_EXP), lambda b, tok, be, nu: (be[b], 0, 0), pipeline_mode=once),
                pl.BlockSpec((None, D_FF_EXP, D_MODEL), lambda b, tok, be, nu: (be[b], 0, 0), pipeline_mode=once),
            ],
            out_specs=pl.BlockSpec((MOE_ROWS, D_MODEL), lambda b, tok, be, nu: (b, 0)),
            scratch_shapes=[pltpu.VMEM((2, MOE_ROWS, D_MODEL), F32), pltpu.SemaphoreType.DMA((2,))],
        ),
        out_shape=jax.ShapeDtypeStruct((ns, D_MODEL), F32),
        compiler_params=_params(("arbitrary",)),
        name="moe_mlp",
    )(slot_tok, blk_exp, nused, xn, w1, w3, w2)


def _combine_kernel(pos_ref, x_ref, r_ref, o_hbm, y_ref, buf, sems, *, rows):
    i = pl.program_id(0)
    cur = lax.rem(i, 2)

    def gather(step, slot, fn):
        def body(r, carry):
            for k in range(2):
                fn(_row_copy(o_hbm, pos_ref[2 * (step * rows + r) + k], buf.at[slot, k], r, sems.at[slot]))
            return carry

        lax.fori_loop(0, rows, body, 0, unroll=4)

    @pl.when(i == 0)
    def _():
        gather(0, 0, lambda cp: cp.start())

    @pl.when(i + 1 < pl.num_programs(0))
    def _():
        gather(i + 1, 1 - cur, lambda cp: cp.start())

    gather(i, cur, lambda cp: cp.wait())
    r = r_ref[...]
    y_ref[...] = x_ref[...] + (buf[cur, 0] * r[:, 2:3] + buf[cur, 1] * r[:, 3:4])


def _combine(x1, route, o, pos, rows):
    T, D = x1.shape
    return pl.pallas_call(
        functools.partial(_combine_kernel, rows=rows),
        grid_spec=pltpu.PrefetchScalarGridSpec(
            num_scalar_prefetch=1,
            grid=(T // rows,),
            in_specs=[pl.BlockSpec((rows, D), lambda i, pos: (i, 0)),
                      pl.BlockSpec((rows, LANES), lambda i, pos: (i, 0)),
                      pl.BlockSpec(memory_space=pl.ANY)],
            out_specs=pl.BlockSpec((rows, D), lambda i, pos: (i, 0)),
            scratch_shapes=[pltpu.VMEM((2, 2, rows, D), F32), pltpu.SemaphoreType.DMA((2,))],
        ),
        out_shape=jax.ShapeDtypeStruct((T, D), F32),
        compiler_params=_params(("arbitrary",)),
        name="moe_combine",
    )(pos, x1, route, o)


def _dispatch_plan(route, T):
    i32 = jnp.int32
    tk = 2 * T
    nb = tk // MOE_ROWS + N_EXP
    experts = jnp.arange(N_EXP, dtype=i32)
    e_flat = route[:, 0:2].astype(i32).reshape(-1)
    order = jnp.argsort(e_flat).astype(i32)
    rank = jnp.argsort(order).astype(i32)
    counts = jnp.sum(e_flat[:, None] == experts[None, :], axis=0, dtype=i32)
    pcounts = (counts + MOE_ROWS - 1) // MOE_ROWS * MOE_ROWS
    pends = jnp.cumsum(pcounts)
    pstarts = pends - pcounts
    ustarts = jnp.cumsum(counts) - counts
    pos = pstarts[e_flat] + rank - ustarts[e_flat]
    slot = jnp.arange(nb * MOE_ROWS, dtype=i32)
    slot_exp = jnp.minimum(jnp.sum(pends[None, :] <= slot[:, None], axis=1, dtype=i32), N_EXP - 1)
    within = slot - pstarts[slot_exp]
    src = jnp.clip(ustarts[slot_exp] + within, 0, tk - 1)
    slot_tok = jnp.where(within < counts[slot_exp], order[src] // 2, 0).astype(i32)
    blk_exp = slot_exp[::MOE_ROWS]
    nused = (pends[-1:] // MOE_ROWS).astype(i32)
    return slot_tok, blk_exp, nused, pos.astype(i32)


def _tile(n, pref):
    return pref if n % pref == 0 else n


def _trunk(x, p):
    B, S, D = x.shape
    T = B * S
    xt = x.reshape(T, D)
    tm = _tile(T, PROJ_TM)
    tn = PROJ_TN
    tr = _tile(T, ROW_TM)

    xn = _rmsnorm(xt, p["norm_mix_g"], tr)
    qk = _proj(xn, p["w_main"], 2 * W_A, BF16, tm, tn, col_off=0, vec=p["qk_gain"], kind="headnorm",
               name="proj_qk")
    vqkv = _proj(xn, p["w_main"], W_A + 2 * QK_B + W_B, BF16, tm, tn, col_off=2 * W_A, name="proj_vqkv")
    om = _proj(xn, p["w_main"], W_B, F32, tm, tn, col_off=3 * W_A + 2 * QK_B + W_B, name="proj_o")
    gates = _proj(xn, p["w_gate"], 2 * D_MODEL, F32, tm, tn, name="proj_gates")
    gif = _proj(xn, p["w_gif"], LANES, F32, tm, LANES, vec=p["b_gif"], kind="bias", name="proj_gif")

    ya = _diff_attention(qk, vqkv, p["slopes"], p["lam"], p["subln_scaled"], B, S,
                         _tile(S, ATTN_TQ), _tile(S, ATTN_TK))

    g4 = gif[:, :4 * N_HEADS_B].reshape(B, S, 4, N_HEADS_B)
    gcol = g4.transpose(0, 3, 1, 2)
    grow = g4.transpose(0, 3, 2, 1)
    hf, hb = _mlstm(vqkv, gcol, grow, B, S)
    yb = _mlstm_out(hf, hb, om, p["mlstm_norm_g"], tr)

    merged = _merge(ya, yb, p["w_branch_a"], p["w_branch_b"], gates, tm, MERGE_TN)
    x1 = _outproj(merged, p["w_out"], xt, tm, MERGE_TN)

    xn2, route = _router(x1, p["norm_ffn_g"], p["w_router"], p["b_router"], tr)
    slot_tok, blk_exp, nused, pos = _dispatch_plan(route, T)
    o = _moe_mlp(xn2, slot_tok, p["w1"], p["w3"], p["w2"], blk_exp, nused)
    y = _combine(x1, route, o, pos, _tile(T, 128))
    return y.reshape(B, S, D)


def _prepare(norm_mix_g, w_in, b_mgate, attn_q_norm_g, attn_k_norm_g, lambda_q1, lambda_k1, lambda_q2,
             lambda_k2, attn_subln_g, mlstm_norm_g, w_branch_a, w_branch_b, w_out, norm_ffn_g,
             w_router_group, b_router_group, w_router_expert, b_router_expert, w_exp_gate, w_exp_up,
             w_exp_down):
    l = 0
    n_main = 3 * W_A + 2 * QK_B + 2 * W_B
    n_if = 4 * N_HEADS_B
    w = w_in[l]
    q_gain = jnp.tile((attn_q_norm_g[l].astype(F32) * (HEAD_DIM_A ** -0.5 * LOG2E)).reshape(-1), N_HEADS_A)
    k_gain = jnp.tile(attn_k_norm_g[l].astype(F32).reshape(-1), N_HEADS_A)
    lam = (jnp.exp(jnp.sum(lambda_q1[l].astype(F32) * lambda_k1[l].astype(F32)))
           - jnp.exp(jnp.sum(lambda_q2[l].astype(F32) * lambda_k2[l].astype(F32))) + LAM_INIT)
    slopes = LOG2E * 2.0 ** (-8.0 * (jnp.arange(N_HEADS_A, dtype=F32) + 1.0) / N_HEADS_A)
    w_router = jnp.zeros((D_MODEL, LANES), F32)
    w_router = w_router.at[:, :N_GROUPS].set(w_router_group[l])
    w_router = w_router.at[:, N_GROUPS:N_GROUPS + N_EXP].set(w_router_expert[l])
    b_router = jnp.zeros((1, LANES), F32)
    b_router = b_router.at[0, :N_GROUPS].set(b_router_group[l].astype(F32))
    b_router = b_router.at[0, N_GROUPS:N_GROUPS + N_EXP].set(b_router_expert[l].astype(F32).reshape(-1))
    return {
        "norm_mix_g": norm_mix_g[l],
        "w_main": w[:, :n_main].astype(BF16),
        "w_gate": w[:, n_main + n_if:].astype(BF16),
        "w_gif": jnp.pad(w[:, n_main:n_main + n_if], ((0, 0), (0, LANES - n_if))).astype(BF16),
        "b_gif": jnp.pad(b_mgate[l].astype(F32), (0, LANES - n_if)),
        "qk_gain": jnp.concatenate([q_gain, k_gain]),
        "lam": lam.reshape(1),
        "slopes": slopes,
        "subln_scaled": attn_subln_g[l].astype(F32) * (1.0 - LAM_INIT),
        "mlstm_norm_g": mlstm_norm_g[l],
        "w_branch_a": w_branch_a[l].astype(BF16),
        "w_branch_b": w_branch_b[l].astype(BF16),
        "w_out": w_out[l].astype(BF16),
        "norm_ffn_g": norm_ffn_g[l],
        "w_router": w_router,
        "b_router": b_router,
        "w1": w_exp_gate[l].astype(BF16),
        "w3": w_exp_up[l].astype(BF16),
        "w2": w_exp_down[l].astype(BF16),
    }


def kernel(x_prompt, x_sample, norm_mix_g, w_in, b_mgate, attn_q_norm_g, attn_k_norm_g, lambda_q1, lambda_k1, lambda_q2, lambda_k2, attn_subln_g, mlstm_norm_g, w_branch_a, w_branch_b, w_out, norm_ffn_g, w_router_group, b_router_group, w_router_expert, b_router_expert, w_exp_gate, w_exp_up, w_exp_down):
    p = _prepare(norm_mix_g, w_in, b_mgate, attn_q_norm_g, attn_k_norm_g, lambda_q1, lambda_k1, lambda_q2,
                 lambda_k2, attn_subln_g, mlstm_norm_g, w_branch_a, w_branch_b, w_out, norm_ffn_g,
                 w_router_group, b_router_group, w_router_expert, b_router_expert, w_exp_gate, w_exp_up,
                 w_exp_down)
    return (_trunk(x_prompt, p), _trunk(x_sample, p))
```

```python
import functools

import jax
import jax.numpy as jnp
from jax import lax
from jax.experimental import pallas as pl
from jax.experimental.pallas import tpu as pltpu

D_MODEL = 4096
HEAD_DIM_A = 128
N_HEADS_A = 8
W_A = 2048
N_HEADS_B = 4
DV_B = 512
DK_B = 256
W_B = 2048
QK_B = 1024
CHUNK = 128
N_GROUPS = 4
EXP_PER_GROUP = 8
N_EXP = 32
D_FF_EXP = 1024
EPS = 1e-6
NEG = -1e30
LAM_INIT = 0.8 - 0.6 * 1.0

LANES = 128
MOE_ROWS = 256
PROJ_TM, PROJ_TN = 1024, 1024
MERGE_TN = 512
ROW_TM = 256
ATTN_TQ, ATTN_TK = 512, 2048
VMEM_LIMIT = 56 * 1024 * 1024
ATTN_ROWS = 16
LOG2E = 1.4426950408889634

F32 = jnp.float32
BF16 = jnp.bfloat16
HIGHEST = lax.Precision.HIGHEST


def _params(sem, vmem=VMEM_LIMIT):
    return pltpu.CompilerParams(dimension_semantics=sem, vmem_limit_bytes=vmem)


def _rmsnorm_kernel(x_ref, g_ref, o_ref):
    x = x_ref[...]
    ms = jnp.mean(x * x, axis=-1, keepdims=True)
    o_ref[...] = (x * lax.rsqrt(ms + EPS) * g_ref[...]).astype(o_ref.dtype)


def _rmsnorm(x, g, tm):
    T, D = x.shape
    return pl.pallas_call(
        _rmsnorm_kernel,
        grid=(T // tm,),
        in_specs=[pl.BlockSpec((tm, D), lambda i: (i, 0)), pl.BlockSpec((1, D), lambda i: (0, 0))],
        out_specs=pl.BlockSpec((tm, D), lambda i: (i, 0)),
        out_shape=jax.ShapeDtypeStruct((T, D), BF16),
        compiler_params=_params(("parallel",)),
        name="rmsnorm",
    )(x, g.reshape(1, D))


def _proj_kernel(a_ref, w_ref, o_ref):
    o_ref[...] = jnp.dot(a_ref[...], w_ref[...], preferred_element_type=F32).astype(o_ref.dtype)


def _proj_bias_kernel(a_ref, w_ref, b_ref, o_ref):
    acc = jnp.dot(a_ref[...], w_ref[...], preferred_element_type=F32)
    o_ref[...] = (acc + b_ref[...]).astype(o_ref.dtype)


def _proj_headnorm_kernel(a_ref, w_ref, g_ref, o_ref, *, transposed):
    acc = jnp.dot(a_ref[...], w_ref[...], preferred_element_type=F32)
    for j in range(acc.shape[1] // HEAD_DIM_A):
        sl = slice(j * HEAD_DIM_A, (j + 1) * HEAD_DIM_A)
        blk = acc[:, sl]
        ms = jnp.mean(blk * blk, axis=-1, keepdims=True)
        y = blk * lax.rsqrt(ms + EPS) * g_ref[:, sl]
        if transposed:
            o_ref[sl, :] = y.T.astype(o_ref.dtype)
        else:
            o_ref[:, sl] = y.astype(o_ref.dtype)


def _proj(a, w, n_out, out_dtype, tm, tn, *, col_off=0, vec=None, kind="plain", name="proj"):
    T, K = a.shape
    assert n_out % tn == 0 and col_off % tn == 0 and T % tm == 0
    off = col_off // tn
    in_specs = [pl.BlockSpec((tm, K), lambda i, j: (i, 0)),
                pl.BlockSpec((K, tn), lambda i, j: (0, j + off))]
    args = [a, w]
    if kind == "plain":
        body = _proj_kernel
    else:
        body = (_proj_bias_kernel if kind == "bias" else
                functools.partial(_proj_headnorm_kernel, transposed=(kind == "headnorm_t")))
        in_specs.append(pl.BlockSpec((1, tn), lambda i, j: (0, j)))
        args.append(vec.reshape(1, n_out).astype(F32))
    if kind == "headnorm_t":
        out_spec = pl.BlockSpec((tn, tm), lambda i, j: (j, i))
        out_shape = jax.ShapeDtypeStruct((n_out, T), out_dtype)
    else:
        out_spec = pl.BlockSpec((tm, tn), lambda i, j: (i, j))
        out_shape = jax.ShapeDtypeStruct((T, n_out), out_dtype)
    return pl.pallas_call(
        body,
        grid=(T // tm, n_out // tn),
        in_specs=in_specs,
        out_specs=out_spec,
        out_shape=out_shape,
        compiler_params=_params(("parallel", "arbitrary")),
        name=name,
    )(*args)


def _attn_kernel(slopes_ref, lam_ref, q_ref, kt_ref, v_ref, g_ref, o_ref,
                 s_sc, p_sc, pm_sc, ps_sc, e_sc, m_sc, l_sc, acc_sc, *, tq, tk):
    h = pl.program_id(1)
    qi = pl.program_id(2)
    ki = pl.program_id(3)
    rows_per = ATTN_ROWS
    n_lane_tiles = tk // LANES

    @pl.when(ki == 0)
    def _():
        m_sc[...] = jnp.full(m_sc.shape, NEG, F32)
        l_sc[...] = jnp.zeros(l_sc.shape, F32)
        acc_sc[...] = jnp.zeros(acc_sc.shape, F32)

    slope = slopes_ref[h]
    off = qi * tq - ki * tk
    left = off >= tk - 1
    right = off <= 1 - tq
    separable = left | right
    col = lax.broadcasted_iota(jnp.int32, (1, tk), 1)
    row = lax.broadcasted_iota(jnp.int32, (rows_per, 1), 0)
    all_rows = lax.broadcasted_iota(jnp.int32, (tq, 1), 0)
    col_bias = slope * jnp.where(left, tk - 1 - col, col).astype(F32)
    row_term = jnp.where(left, all_rows + (off - (tk - 1)), -off - all_rows).astype(F32)
    shift = jnp.where(separable, -slope * row_term, 0.0)

    def lane_tile(j):
        return slice(j * LANES, (j + 1) * LANES)

    def row_group(i):
        return slice(i * rows_per, (i + 1) * rows_per)

    def pass_a(c, is_separable):
        for i in range(tq // rows_per):
            rows = row_group(i)
            if is_separable:
                t = s_sc[c, rows, :] - col_bias
            else:
                d = (row + (i * rows_per + off)) - col
                t = s_sc[c, rows, :] - slope * jnp.abs(d.astype(F32))
                s_sc[c, rows, :] = t
            pm = t[:, lane_tile(0)]
            for j in range(1, n_lane_tiles):
                pm = jnp.maximum(pm, t[:, lane_tile(j)])
            pm_sc[c, rows, :] = pm

    def pass_b(c, is_separable):
        for i in range(tq // rows_per):
            rows = row_group(i)
            e = e_sc[c, rows, :]
            ps = None
            for j in range(n_lane_tiles):
                t = s_sc[c, rows, lane_tile(j)]
                if is_separable:
                    t = t - col_bias[:, lane_tile(j)]
                p = jnp.exp2(t - e)
                p_sc[c, rows, lane_tile(j)] = p.astype(BF16)
                ps = p if ps is None else ps + p
            ps_sc[c, rows, :] = ps

    def tile(is_separable):
        v = v_ref[...]
        for c in range(2):
            sl = slice(c * HEAD_DIM_A, (c + 1) * HEAD_DIM_A)
            s_sc[c] = jnp.dot(q_ref[:, sl], kt_ref[sl, :], preferred_element_type=F32)
        for c in range(2):
            pass_a(c, is_separable)
            m_prev = m_sc[c]
            m_new = jnp.maximum(m_prev, jnp.max(pm_sc[c], axis=-1, keepdims=True) + shift)
            alpha = jnp.exp2(m_prev - m_new)
            m_sc[c] = m_new
            e_sc[c] = jnp.broadcast_to(m_new - shift, (tq, LANES))
            pass_b(c, is_separable)
            l_sc[c] = alpha * l_sc[c] + jnp.sum(ps_sc[c], axis=-1, keepdims=True)
            acc_sc[c] = alpha * acc_sc[c] + jnp.dot(p_sc[c], v, preferred_element_type=F32)

    @pl.when(separable)
    def _():
        tile(True)

    @pl.when(jnp.logical_not(separable))
    def _():
        tile(False)

    @pl.when(ki == pl.num_programs(3) - 1)
    def _():
        o = acc_sc[0] / l_sc[0] - lam_ref[0] * (acc_sc[1] / l_sc[1])
        ms = jnp.mean(o * o, axis=-1, keepdims=True)
        o_ref[...] = (o * lax.rsqrt(ms + EPS) * g_ref[...]).astype(o_ref.dtype)


def _diff_attention(q, kt, vqkv, slopes, lam, subln_scaled, B, S, tq, tk):
    T = B * S
    nq, nk = S // tq, S // tk
    hw = 2 * HEAD_DIM_A
    return pl.pallas_call(
        functools.partial(_attn_kernel, tq=tq, tk=tk),
        grid=(B, N_HEADS_A, nq, nk),
        in_specs=[
            pl.BlockSpec(memory_space=pltpu.SMEM),
            pl.BlockSpec(memory_space=pltpu.SMEM),
            pl.BlockSpec((tq, hw), lambda b, h, i, j: (b * nq + i, h)),
            pl.BlockSpec((hw, tk), lambda b, h, i, j: (h, b * nk + j)),
            pl.BlockSpec((tk, hw), lambda b, h, i, j: (b * nk + j, h)),
            pl.BlockSpec((1, hw), lambda b, h, i, j: (0, 0)),
        ],
        out_specs=pl.BlockSpec((tq, hw), lambda b, h, i, j: (b * nq + i, h)),
        out_shape=jax.ShapeDtypeStruct((T, W_A), BF16),
        scratch_shapes=[pltpu.VMEM((2, tq, tk), F32), pltpu.VMEM((2, tq, tk), BF16),
                        pltpu.VMEM((2, tq, LANES), F32), pltpu.VMEM((2, tq, LANES), F32),
                        pltpu.VMEM((2, tq, LANES), F32),
                        pltpu.VMEM((2, tq, 1), F32), pltpu.VMEM((2, tq, 1), F32),
                        pltpu.VMEM((2, tq, hw), F32)],
        compiler_params=_params(("parallel", "parallel", "parallel", "arbitrary")),
        name="diff_attention",
    )(slopes, lam, q, kt, vqkv, subln_scaled.reshape(1, hw))


def _log_sigmoid(x):
    return jnp.minimum(x, 0.0) - jnp.log1p(jnp.exp(-jnp.abs(x)))


def _mlstm_chunk(q, k, v, gcol, grow, c_ref, n_ref, m_ref, reverse):
    L = q.shape[0]
    ii, fi = (2, 3) if reverse else (0, 1)
    r = lax.broadcasted_iota(jnp.int32, (L, L), 0)
    c = lax.broadcasted_iota(jnp.int32, (L, L), 1)
    lower = c <= r
    upper = c >= r
    tri_lo = jnp.where(lower, 1.0, 0.0).astype(F32)
    tri_up = jnp.where(upper, 1.0, 0.0).astype(F32)
    ic_row = grow[ii:ii + 1, :]
    ic_col = gcol[:, ii:ii + 1]
    lf_row = _log_sigmoid(grow[fi:fi + 1, :])
    lf_col = _log_sigmoid(gcol[:, fi:fi + 1])
    if reverse:
        b_row = jnp.dot(lf_row, tri_lo, precision=HIGHEST, preferred_element_type=F32)
        b_col = jnp.dot(tri_up, lf_col, precision=HIGHEST, preferred_element_type=F32)
        mask = upper
        b_last = b_row[:, 0:1]
    else:
        b_row = jnp.dot(lf_row, tri_up, precision=HIGHEST, preferred_element_type=F32)
        b_col = jnp.dot(tri_lo, lf_col, precision=HIGHEST, preferred_element_type=F32)
        mask = lower
        b_last = b_row[:, L - 1:L]
    m = m_ref[...]
    log_d = jnp.where(mask, b_col - b_row + ic_row, NEG)
    inter = b_col + m
    m_row = jnp.maximum(inter, jnp.max(log_d, axis=-1, keepdims=True))
    d_m = jnp.exp(log_d - m_row)
    sc = jnp.exp(inter - m_row)
    qs = q * (DK_B ** -0.5)
    s = lax.dot_general(qs, k, (((1,), (1,)), ((), ())), preferred_element_type=F32) * d_m
    c_state = c_ref[...]
    n_state = n_ref[...]
    num = (jnp.dot(s.astype(BF16), v, preferred_element_type=F32)
           + sc * jnp.dot(qs, c_state.astype(BF16), preferred_element_type=F32))
    den = (jnp.sum(s, axis=-1, keepdims=True)
           + sc * jnp.sum(qs.astype(F32) * n_state, axis=-1, keepdims=True))
    h = num / jnp.maximum(jnp.abs(den), jnp.exp(-m_row))
    w_col = b_last - b_col + ic_col
    w_row = b_last - b_row + ic_row
    m_new = jnp.maximum(b_last + m, jnp.max(w_row, axis=-1, keepdims=True))
    decay = jnp.exp(b_last + m - m_new)
    ws_col = jnp.exp(w_col - m_new)
    wv = (ws_col * v.astype(F32)).astype(BF16)
    c_ref[...] = decay * c_state + lax.dot_general(k, wv, (((0,), (0,)), ((), ())),
                                                   preferred_element_type=F32)
    n_ref[...] = decay * n_state + jnp.sum(ws_col * k.astype(F32), axis=0, keepdims=True)
    m_ref[...] = m_new
    return h


def _mlstm_kernel(qf_ref, kf_ref, vf_ref, gcf_ref, grf_ref, qb_ref, kb_ref, vb_ref, gcb_ref, grb_ref,
                  hf_ref, hb_ref, c_sc, n_sc, m_sc):
    @pl.when(pl.program_id(1) == 0)
    def _():
        c_sc[...] = jnp.zeros(c_sc.shape, F32)
        n_sc[...] = jnp.zeros(n_sc.shape, F32)
        m_sc[...] = jnp.full(m_sc.shape, NEG, F32)

    for h in range(N_HEADS_B):
        qk = slice(h * DK_B, (h + 1) * DK_B)
        vv = slice(h * DV_B, (h + 1) * DV_B)
        hf_ref[:, vv] = _mlstm_chunk(qf_ref[:, qk], kf_ref[:, qk], vf_ref[:, vv], gcf_ref[h], grf_ref[h],
                                     c_sc.at[0, h], n_sc.at[0, h], m_sc.at[0, h], reverse=False)
        hb_ref[:, vv] = _mlstm_chunk(qb_ref[:, qk], kb_ref[:, qk], vb_ref[:, vv], gcb_ref[h], grb_ref[h],
                                     c_sc.at[1, h], n_sc.at[1, h], m_sc.at[1, h], reverse=True)


def _mlstm(vqkv, gcol, grow, B, S):
    T = B * S
    nc = S // CHUNK
    q_off = W_A // QK_B
    k_off = (W_A + QK_B) // QK_B
    v_off = (W_A + 2 * QK_B) // W_B

    def fwd(b, c):
        return b * nc + c

    def bwd(b, c):
        return b * nc + nc - 1 - c

    def specs(pos):
        return [
            pl.BlockSpec((CHUNK, QK_B), lambda b, c: (pos(b, c), q_off)),
            pl.BlockSpec((CHUNK, QK_B), lambda b, c: (pos(b, c), k_off)),
            pl.BlockSpec((CHUNK, W_B), lambda b, c: (pos(b, c), v_off)),
            pl.BlockSpec((None, N_HEADS_B, CHUNK, 4), lambda b, c: (b, 0, pos(b, c) - b * nc, 0)),
            pl.BlockSpec((None, N_HEADS_B, 4, CHUNK), lambda b, c: (b, 0, 0, pos(b, c) - b * nc)),
        ]

    out_sds = jax.ShapeDtypeStruct((T, W_B), F32)
    return pl.pallas_call(
        _mlstm_kernel,
        grid=(B, nc),
        in_specs=specs(fwd) + specs(bwd),
        out_specs=[pl.BlockSpec((CHUNK, W_B), lambda b, c: (fwd(b, c), 0)),
                   pl.BlockSpec((CHUNK, W_B), lambda b, c: (bwd(b, c), 0))],
        out_shape=[out_sds, out_sds],
        scratch_shapes=[pltpu.VMEM((2, N_HEADS_B, DK_B, DV_B), F32), pltpu.VMEM((2, N_HEADS_B, 1, DK_B), F32),
                        pltpu.VMEM((2, N_HEADS_B, 1, 1), F32)],
        compiler_params=_params(("parallel", "arbitrary")),
        name="mlstm",
    )(vqkv, vqkv, vqkv, gcol, grow, vqkv, vqkv, vqkv, gcol, grow)


def _mlstm_out_kernel(hf_ref, hb_ref, o_ref, g_ref, y_ref):
    for j in range(N_HEADS_B):
        sl = slice(j * DV_B, (j + 1) * DV_B)
        h = hf_ref[:, sl] + hb_ref[:, sl]
        ms = jnp.mean(h * h, axis=-1, keepdims=True)
        hn = h * lax.rsqrt(ms + EPS) * g_ref[...]
        y_ref[:, sl] = (jax.nn.sigmoid(o_ref[:, sl]) * hn).astype(y_ref.dtype)


def _mlstm_out(hf, hb, om, norm_g, tm):
    T = hf.shape[0]
    row = pl.BlockSpec((tm, W_B), lambda i: (i, 0))
    return pl.pallas_call(
        _mlstm_out_kernel,
        grid=(T // tm,),
        in_specs=[row, row, row, pl.BlockSpec((1, DV_B), lambda i: (0, 0))],
        out_specs=row,
        out_shape=jax.ShapeDtypeStruct((T, W_B), BF16),
        compiler_params=_params(("parallel",)),
        name="mlstm_out",
    )(hf, hb, om, norm_g.reshape(1, DV_B))


def _merge_kernel(ya_ref, yb_ref, wa_ref, wb_ref, ga_ref, gb_ref, o_ref):
    pa = jnp.dot(ya_ref[...], wa_ref[...], preferred_element_type=F32)
    pb = jnp.dot(yb_ref[...], wb_ref[...], preferred_element_type=F32)
    o_ref[...] = (jax.nn.sigmoid(ga_ref[...]) * pa + jax.nn.sigmoid(gb_ref[...]) * pb).astype(o_ref.dtype)


def _merge(ya, yb, wa, wb, gates, tm, tn):
    T = ya.shape[0]
    nb = D_MODEL // tn
    return pl.pallas_call(
        _merge_kernel,
        grid=(T // tm, nb),
        in_specs=[pl.BlockSpec((tm, W_A), lambda i, j: (i, 0)),
                  pl.BlockSpec((tm, W_B), lambda i, j: (i, 0)),
                  pl.BlockSpec((W_A, tn), lambda i, j: (0, j)),
                  pl.BlockSpec((W_B, tn), lambda i, j: (0, j)),
                  pl.BlockSpec((tm, tn), lambda i, j: (i, j)),
                  pl.BlockSpec((tm, tn), lambda i, j: (i, nb + j))],
        out_specs=pl.BlockSpec((tm, tn), lambda i, j: (i, j)),
        out_shape=jax.ShapeDtypeStruct((T, D_MODEL), BF16),
        compiler_params=_params(("parallel", "arbitrary")),
        name="merge",
    )(ya, yb, wa, wb, gates, gates)


def _outproj_kernel(a_ref, w_ref, x_ref, o_ref):
    o_ref[...] = x_ref[...] + jnp.dot(a_ref[...], w_ref[...], preferred_element_type=F32)


def _outproj(merged, w, x, tm, tn):
    T = merged.shape[0]
    return pl.pallas_call(
        _outproj_kernel,
        grid=(T // tm, D_MODEL // tn),
        in_specs=[pl.BlockSpec((tm, D_MODEL), lambda i, j: (i, 0)),
                  pl.BlockSpec((D_MODEL, tn), lambda i, j: (0, j)),
                  pl.BlockSpec((tm, tn), lambda i, j: (i, j))],
        out_specs=pl.BlockSpec((tm, tn), lambda i, j: (i, j)),
        out_shape=jax.ShapeDtypeStruct((T, D_MODEL), F32),
        compiler_params=_params(("parallel", "arbitrary")),
        name="outproj",
    )(merged, w, x)


def _router_kernel(x_ref, g_ref, wr_ref, br_ref, xn_ref, r_ref):
    x = x_ref[...]
    ms = jnp.mean(x * x, axis=-1, keepdims=True)
    xn = x * lax.rsqrt(ms + EPS) * g_ref[...]
    xn_ref[...] = xn
    logits = jnp.dot(xn, wr_ref[...], precision=HIGHEST, preferred_element_type=F32) + br_ref[...]
    lane = lax.broadcasted_iota(jnp.int32, logits.shape, 1)
    cm = jnp.where(lane < N_GROUPS, logits, NEG)
    cmax = jnp.max(cm, axis=-1, keepdims=True)
    p_top = 1.0 / jnp.sum(jnp.exp(cm - cmax), axis=-1, keepdims=True)
    grp = jnp.min(jnp.where(cm == cmax, lane, LANES), axis=-1, keepdims=True)
    lo = N_GROUPS + EXP_PER_GROUP * grp
    fm = jnp.where((lane >= lo) & (lane < lo + EXP_PER_GROUP), logits, NEG)
    fmax = jnp.max(fm, axis=-1, keepdims=True)
    fe = jnp.exp(fm - fmax)
    pe = fe / jnp.sum(fe, axis=-1, keepdims=True)
    i1 = jnp.min(jnp.where(fm == fmax, lane, LANES), axis=-1, keepdims=True)
    p1 = jnp.sum(jnp.where(lane == i1, pe, 0.0), axis=-1, keepdims=True)
    fm2 = jnp.where(lane == i1, NEG, fm)
    f2max = jnp.max(fm2, axis=-1, keepdims=True)
    i2 = jnp.min(jnp.where(fm2 == f2max, lane, LANES), axis=-1, keepdims=True)
    p2 = jnp.sum(jnp.where(lane == i2, pe, 0.0), axis=-1, keepdims=True)
    g1 = p_top * (p1 / (p1 + p2))
    g2 = p_top * (p2 / (p1 + p2))
    e1 = (i1 - N_GROUPS).astype(F32)
    e2 = (i2 - N_GROUPS).astype(F32)
    r_ref[...] = jnp.where(lane == 0, e1, jnp.where(lane == 1, e2, jnp.where(lane == 2, g1,
                           jnp.where(lane == 3, g2, 0.0))))


def _router(x1, g, wr, br, tm):
    T = x1.shape[0]
    return pl.pallas_call(
        _router_kernel,
        grid=(T // tm,),
        in_specs=[pl.BlockSpec((tm, D_MODEL), lambda i: (i, 0)),
                  pl.BlockSpec((1, D_MODEL), lambda i: (0, 0)),
                  pl.BlockSpec((D_MODEL, LANES), lambda i: (0, 0)),
                  pl.BlockSpec((1, LANES), lambda i: (0, 0))],
        out_specs=[pl.BlockSpec((tm, D_MODEL), lambda i: (i, 0)),
                   pl.BlockSpec((tm, LANES), lambda i: (i, 0))],
        out_shape=[jax.ShapeDtypeStruct((T, D_MODEL), F32), jax.ShapeDtypeStruct((T, LANES), F32)],
        compiler_params=_params(("parallel",)),
        name="router",
    )(x1, g.reshape(1, D_MODEL), wr, br)


def _row_copy(src_hbm, row, dst, r, sem):
    return pltpu.make_async_copy(src_hbm.at[pl.ds(row, 1)], dst.at[pl.ds(r, 1)], sem)


def _moe_kernel(tok_ref, be_ref, nused_ref, x_hbm, w1_ref, w3_ref, w2_ref, o_ref, xbuf, sems):
    b = pl.program_id(0)
    nused = nused_ref[0]
    cur = lax.rem(b, 2)

    def gather(blk, buf, fn):
        def body(r, carry):
            fn(_row_copy(x_hbm, tok_ref[blk * MOE_ROWS + r], xbuf.at[buf], r, sems.at[buf]))
            return carry

        lax.fori_loop(0, MOE_ROWS, body, 0, unroll=8)

    @pl.when(b == 0)
    def _():
        gather(0, 0, lambda cp: cp.start())

    @pl.when(b + 1 < nused)
    def _():
        gather(b + 1, 1 - cur, lambda cp: cp.start())

    @pl.when(b < nused)
    def _():
        gather(b, cur, lambda cp: cp.wait())
        x = xbuf[cur].astype(BF16)
        h1 = jnp.dot(x, w1_ref[...], preferred_element_type=F32)
        h3 = jnp.dot(x, w3_ref[...], preferred_element_type=F32)
        hid = (h1 * jax.nn.sigmoid(h1)) * h3
        o_ref[...] = jnp.dot(hid.astype(BF16), w2_ref[...], preferred_element_type=F32)

    @pl.when(b >= nused)
    def _():
        o_ref[...] = jnp.zeros(o_ref.shape, o_ref.dtype)


def _moe_mlp(xn, slot_tok, w1, w3, w2, blk_exp, nused):
    ns = slot_tok.shape[0]
    nb = ns // MOE_ROWS
    once = pl.Buffered(1)
    return pl.pallas_call(
        _moe_kernel,
        grid_spec=pltpu.PrefetchScalarGridSpec(
            num_scalar_prefetch=3,
            grid=(nb,),
            in_specs=[
                pl.BlockSpec(memory_space=pl.ANY),
                pl.BlockSpec((None, D_MODEL, D_FF_EXP), lambda b, tok, be, nu: (be[b], 0, 0), pipeline_mode=once),
                pl.BlockSpec((None, D_MODEL, D_FF_EXP), lambda b, tok, be, nu: (be[b], 0, 0), pipeline_mode=once),
                pl.BlockSpec((None, D_FF_EXP, D_MODEL), lambda b, tok, be, nu: (be[b], 0, 0), pipeline_mode=once),
            ],
            out_specs=pl.BlockSpec((MOE_ROWS, D_MODEL), lambda b, tok, be, nu: (b, 0)),
            scratch_shapes=[pltpu.VMEM((2, MOE_ROWS, D_MODEL), F32), pltpu.SemaphoreType.DMA((2,))],
        ),
        out_shape=jax.ShapeDtypeStruct((ns, D_MODEL), F32),
        compiler_params=_params(("arbitrary",)),
        name="moe_mlp",
    )(slot_tok, blk_exp, nused, xn, w1, w3, w2)


def _combine_kernel(pos_ref, x_ref, r_ref, o_hbm, y_ref, buf, sems, *, rows):
    i = pl.program_id(0)
    cur = lax.rem(i, 2)

    def gather(step, slot, fn):
        def body(r, carry):
            for k in range(2):
                fn(_row_copy(o_hbm, pos_ref[2 * (step * rows + r) + k], buf.at[slot, k], r, sems.at[slot]))
            return carry

        lax.fori_loop(0, rows, body, 0, unroll=4)

    @pl.when(i == 0)
    def _():
        gather(0, 0, lambda cp: cp.start())

    @pl.when(i + 1 < pl.num_programs(0))
    def _():
        gather(i + 1, 1 - cur, lambda cp: cp.start())

    gather(i, cur, lambda cp: cp.wait())
    r = r_ref[...]
    y_ref[...] = x_ref[...] + (buf[cur, 0] * r[:, 2:3] + buf[cur, 1] * r[:, 3:4])


def _combine(x1, route, o, pos, rows):
    T, D = x1.shape
    return pl.pallas_call(
        functools.partial(_combine_kernel, rows=rows),
        grid_spec=pltpu.PrefetchScalarGridSpec(
            num_scalar_prefetch=1,
            grid=(T // rows,),
            in_specs=[pl.BlockSpec((rows, D), lambda i, pos: (i, 0)),
                      pl.BlockSpec((rows, LANES), lambda i, pos: (i, 0)),
                      pl.BlockSpec(memory_space=pl.ANY)],
            out_specs=pl.BlockSpec((rows, D), lambda i, pos: (i, 0)),
            scratch_shapes=[pltpu.VMEM((2, 2, rows, D), F32), pltpu.SemaphoreType.DMA((2,))],
        ),
        out_shape=jax.ShapeDtypeStruct((T, D), F32),
        compiler_params=_params(("arbitrary",)),
        name="moe_combine",
    )(pos, x1, route, o)


def _dispatch_plan(route, T):
    i32 = jnp.int32
    tk = 2 * T
    nb = tk // MOE_ROWS + N_EXP
    experts = jnp.arange(N_EXP, dtype=i32)
    e_flat = route[:, 0:2].astype(i32).reshape(-1)
    order = jnp.argsort(e_flat).astype(i32)
    rank = jnp.argsort(order).astype(i32)
    counts = jnp.sum(e_flat[:, None] == experts[None, :], axis=0, dtype=i32)
    pcounts = (counts + MOE_ROWS - 1) // MOE_ROWS * MOE_ROWS
    pends = jnp.cumsum(pcounts)
    pstarts = pends - pcounts
    ustarts = jnp.cumsum(counts) - counts
    pos = pstarts[e_flat] + rank - ustarts[e_flat]
    slot = jnp.arange(nb * MOE_ROWS, dtype=i32)
    slot_exp = jnp.minimum(jnp.sum(pends[None, :] <= slot[:, None], axis=1, dtype=i32), N_EXP - 1)
    within = slot - pstarts[slot_exp]
    src = jnp.clip(ustarts[slot_exp] + within, 0, tk - 1)
    slot_tok = jnp.where(within < counts[slot_exp], order[src] // 2, 0).astype(i32)
    blk_exp = slot_exp[::MOE_ROWS]
    nused = (pends[-1:] // MOE_ROWS).astype(i32)
    return slot_tok, blk_exp, nused, pos.astype(i32)


def _tile(n, pref):
    return pref if n % pref == 0 else n


def _trunk(x, p):
    B, S, D = x.shape
    T = B * S
    xt = x.reshape(T, D)
    tm = _tile(T, PROJ_TM)
    tn = PROJ_TN
    tr = _tile(T, ROW_TM)

    xn = _rmsnorm(xt, p["norm_mix_g"], tr)
    q = _proj(xn, p["w_main"], W_A, BF16, tm, tn, col_off=0, vec=p["q_gain"], kind="headnorm", name="proj_q")
    kt = _proj(xn, p["w_main"], W_A, BF16, tm, tn, col_off=W_A, vec=p["k_gain"], kind="headnorm_t",
               name="proj_kt")
    vqkv = _proj(xn, p["w_main"], W_A + 2 * QK_B + W_B, BF16, tm, tn, col_off=2 * W_A, name="proj_vqkv")
    om = _proj(xn, p["w_main"], W_B, F32, tm, tn, col_off=3 * W_A + 2 * QK_B + W_B, name="proj_o")
    gates = _proj(xn, p["w_gate"], 2 * D_MODEL, F32, tm, tn, name="proj_gates")
    gif = _proj(xn, p["w_gif"], LANES, F32, tm, LANES, vec=p["b_gif"], kind="bias", name="proj_gif")

    ya = _diff_attention(q, kt, vqkv, p["slopes"], p["lam"], p["subln_scaled"], B, S,
                         _tile(S, ATTN_TQ), _tile(S, ATTN_TK))

    g4 = gif[:, :4 * N_HEADS_B].reshape(B, S, 4, N_HEADS_B)
    gcol = g4.transpose(0, 3, 1, 2)
    grow = g4.transpose(0, 3, 2, 1)
    hf, hb = _mlstm(vqkv, gcol, grow, B, S)
    yb = _mlstm_out(hf, hb, om, p["mlstm_norm_g"], tr)

    merged = _merge(ya, yb, p["w_branch_a"], p["w_branch_b"], gates, tm, MERGE_TN)
    x1 = _outproj(merged, p["w_out"], xt, tm, MERGE_TN)

    xn2, route = _router(x1, p["norm_ffn_g"], p["w_router"], p["b_router"], tr)
    slot_tok, blk_exp, nused, pos = _dispatch_plan(route, T)
    o = _moe_mlp(xn2, slot_tok, p["w1"], p["w3"], p["w2"], blk_exp, nused)
    y = _combine(x1, route, o, pos, _tile(T, 128))
    return y.reshape(B, S, D)


def _prepare(norm_mix_g, w_in, b_mgate, attn_q_norm_g, attn_k_norm_g, lambda_q1, lambda_k1, lambda_q2,
             lambda_k2, attn_subln_g, mlstm_norm_g, w_branch_a, w_branch_b, w_out, norm_ffn_g,
             w_router_group, b_router_group, w_router_expert, b_router_expert, w_exp_gate, w_exp_up,
             w_exp_down):
    l = 0
    n_main = 3 * W_A + 2 * QK_B + 2 * W_B
    n_if = 4 * N_HEADS_B
    w = w_in[l]
    q_gain = jnp.tile((attn_q_norm_g[l].astype(F32) * (HEAD_DIM_A ** -0.5 * LOG2E)).reshape(-1), N_HEADS_A)
    k_gain = jnp.tile(attn_k_norm_g[l].astype(F32).reshape(-1), N_HEADS_A)
    lam = (jnp.exp(jnp.sum(lambda_q1[l].astype(F32) * lambda_k1[l].astype(F32)))
           - jnp.exp(jnp.sum(lambda_q2[l].astype(F32) * lambda_k2[l].astype(F32))) + LAM_INIT)
    slopes = LOG2E * 2.0 ** (-8.0 * (jnp.arange(N_HEADS_A, dtype=F32) + 1.0) / N_HEADS_A)
    w_router = jnp.zeros((D_MODEL, LANES), F32)
    w_router = w_router.at[:, :N_GROUPS].set(w_router_group[l])
    w_router = w_router.at[:, N_GROUPS:N_GROUPS + N_EXP].set(w_router_expert[l])
    b_router = jnp.zeros((1, LANES), F32)
    b_router = b_router.at[0, :N_GROUPS].set(b_router_group[l].astype(F32))
    b_router = b_router.at[0, N_GROUPS:N_GROUPS + N_EXP].set(b_router_expert[l].astype(F32).reshape(-1))
    return {
        "norm_mix_g": norm_mix_g[l],
        "w_main": w[:, :n_main].astype(BF16),
        "w_gate": w[:, n_main + n_if:].astype(BF16),
        "w_gif": jnp.pad(w[:, n_main:n_main + n_if], ((0, 0), (0, LANES - n_if))).astype(BF16),
        "b_gif": jnp.pad(b_mgate[l].astype(F32), (0, LANES - n_if)),
        "q_gain": q_gain,
        "k_gain": k_gain,
        "lam": lam.reshape(1),
        "slopes": slopes,
        "subln_scaled": attn_subln_g[l].astype(F32) * (1.0 - LAM_INIT),
        "mlstm_norm_g": mlstm_norm_g[l],
        "w_branch_a": w_branch_a[l].astype(BF16),
        "w_branch_b": w_branch_b[l].astype(BF16),
        "w_out": w_out[l].astype(BF16),
        "norm_ffn_g": norm_ffn_g[l],
        "w_router": w_router,
        "b_router": b_router,
        "w1": w_exp_gate[l].astype(BF16),
        "w3": w_exp_up[l].astype(BF16),
        "w2": w_exp_down[l].astype(BF16),
    }


def kernel(x_prompt, x_sample, norm_mix_g, w_in, b_mgate, attn_q_norm_g, attn_k_norm_g, lambda_q1, lambda_k1, lambda_q2, lambda_k2, attn_subln_g, mlstm_norm_g, w_branch_a, w_branch_b, w_out, norm_ffn_g, w_router_group, b_router_group, w_router_expert, b_router_expert, w_exp_gate, w_exp_up, w_exp_down):
    p = _prepare(norm_mix_g, w_in, b_mgate, attn_q_norm_g, attn_k_norm_g, lambda_q1, lambda_k1, lambda_q2,
                 lambda_k2, attn_subln_g, mlstm_norm_g, w_branch_a, w_branch_b, w_out, norm_ffn_g,
                 w_router_group, b_router_group, w_router_expert, b_router_expert, w_exp_gate, w_exp_up,
                 w_exp_down)
    return (_trunk(x_prompt, p), _trunk(x_sample, p))
```

```python
import functools

import jax
import jax.numpy as jnp
from jax import lax
from jax.experimental import pallas as pl
from jax.experimental.pallas import tpu as pltpu

D_MODEL = 4096
HEAD_DIM_A = 128
N_HEADS_A = 8
W_A = 2048
N_HEADS_B = 4
DV_B = 512
DK_B = 256
W_B = 2048
QK_B = 1024
CHUNK = 128
N_GROUPS = 4
EXP_PER_GROUP = 8
N_EXP = 32
D_FF_EXP = 1024
EPS = 1e-6
NEG = -1e30
LAM_INIT = 0.8 - 0.6 * 1.0

LANES = 128
MOE_ROWS = 256
PROJ_TM, PROJ_TN = 1024, 1024
MERGE_TN = 512
ROW_TM = 256
ATTN_TQ, ATTN_TK = 512, 2048
ATTN_UNIT = 256
VMEM_LIMIT = 56 * 1024 * 1024
ATTN_ROWS = 16
LOG2E = 1.4426950408889634

F32 = jnp.float32
BF16 = jnp.bfloat16
HIGHEST = lax.Precision.HIGHEST


def _params(sem, vmem=VMEM_LIMIT):
    return pltpu.CompilerParams(dimension_semantics=sem, vmem_limit_bytes=vmem)


def _rmsnorm_kernel(x_ref, g_ref, o_ref):
    x = x_ref[...]
    ms = jnp.mean(x * x, axis=-1, keepdims=True)
    o_ref[...] = (x * lax.rsqrt(ms + EPS) * g_ref[...]).astype(o_ref.dtype)


def _rmsnorm(x, g, tm):
    T, D = x.shape
    return pl.pallas_call(
        _rmsnorm_kernel,
        grid=(T // tm,),
        in_specs=[pl.BlockSpec((tm, D), lambda i: (i, 0)), pl.BlockSpec((1, D), lambda i: (0, 0))],
        out_specs=pl.BlockSpec((tm, D), lambda i: (i, 0)),
        out_shape=jax.ShapeDtypeStruct((T, D), BF16),
        compiler_params=_params(("parallel",)),
        name="rmsnorm",
    )(x, g.reshape(1, D))


def _proj_kernel(a_ref, w_ref, o_ref):
    o_ref[...] = jnp.dot(a_ref[...], w_ref[...], preferred_element_type=F32).astype(o_ref.dtype)


def _proj_bias_kernel(a_ref, w_ref, b_ref, o_ref):
    acc = jnp.dot(a_ref[...], w_ref[...], preferred_element_type=F32)
    o_ref[...] = (acc + b_ref[...]).astype(o_ref.dtype)


def _proj_headnorm_kernel(a_ref, w_ref, g_ref, o_ref, *, transposed):
    acc = jnp.dot(a_ref[...], w_ref[...], preferred_element_type=F32)
    for j in range(acc.shape[1] // HEAD_DIM_A):
        sl = slice(j * HEAD_DIM_A, (j + 1) * HEAD_DIM_A)
        blk = acc[:, sl]
        ms = jnp.mean(blk * blk, axis=-1, keepdims=True)
        y = blk * lax.rsqrt(ms + EPS) * g_ref[:, sl]
        if transposed:
            o_ref[sl, :] = y.T.astype(o_ref.dtype)
        else:
            o_ref[:, sl] = y.astype(o_ref.dtype)


def _proj(a, w, n_out, out_dtype, tm, tn, *, col_off=0, vec=None, kind="plain", name="proj"):
    T, K = a.shape
    assert n_out % tn == 0 and col_off % tn == 0 and T % tm == 0
    off = col_off // tn
    in_specs = [pl.BlockSpec((tm, K), lambda i, j: (i, 0)),
                pl.BlockSpec((K, tn), lambda i, j: (0, j + off))]
    args = [a, w]
    if kind == "plain":
        body = _proj_kernel
    else:
        body = (_proj_bias_kernel if kind == "bias" else
                functools.partial(_proj_headnorm_kernel, transposed=(kind == "headnorm_t")))
        in_specs.append(pl.BlockSpec((1, tn), lambda i, j: (0, j)))
        args.append(vec.reshape(1, n_out).astype(F32))
    if kind == "headnorm_t":
        out_spec = pl.BlockSpec((tn, tm), lambda i, j: (j, i))
        out_shape = jax.ShapeDtypeStruct((n_out, T), out_dtype)
    else:
        out_spec = pl.BlockSpec((tm, tn), lambda i, j: (i, j))
        out_shape = jax.ShapeDtypeStruct((T, n_out), out_dtype)
    return pl.pallas_call(
        body,
        grid=(T // tm, n_out // tn),
        in_specs=in_specs,
        out_specs=out_spec,
        out_shape=out_shape,
        compiler_params=_params(("parallel", "arbitrary")),
        name=name,
    )(*args)


def _tree_reduce(op, xs):
    while len(xs) > 1:
        xs = [op(xs[k], xs[k + 1]) if k + 1 < len(xs) else xs[k] for k in range(0, len(xs), 2)]
    return xs[0]


def _attn_kernel(slopes_ref, lam_ref, q_ref, kt_ref, v_ref, g_ref, o_ref,
                 s_sc, p_sc, pm_sc, ps_sc, e_sc, m_sc, l_sc, acc_sc, *, tq, tk):
    h = pl.program_id(1)
    qi = pl.program_id(2)
    ki = pl.program_id(3)
    rows_per = ATTN_ROWS
    n_lane_tiles = tk // LANES

    @pl.when(ki == 0)
    def _():
        m_sc[...] = jnp.full(m_sc.shape, NEG, F32)
        l_sc[...] = jnp.zeros(l_sc.shape, F32)
        acc_sc[...] = jnp.zeros(acc_sc.shape, F32)

    slope = slopes_ref[h]
    off = qi * tq - ki * tk
    left = off >= tk - 1
    right = off <= 1 - tq
    separable = left | right
    col = lax.broadcasted_iota(jnp.int32, (1, tk), 1)
    row = lax.broadcasted_iota(jnp.int32, (rows_per, 1), 0)
    all_rows = lax.broadcasted_iota(jnp.int32, (tq, 1), 0)
    col_bias = slope * jnp.where(left, tk - 1 - col, col).astype(F32)
    row_term = jnp.where(left, all_rows + (off - (tk - 1)), -off - all_rows).astype(F32)
    shift = jnp.where(separable, -slope * row_term, 0.0)

    def lane_tile(j):
        return slice(j * LANES, (j + 1) * LANES)

    def row_group(i):
        return slice(i * rows_per, (i + 1) * rows_per)

    def pass_a(c, first, is_separable):
        for i in range(first, first + ATTN_UNIT // rows_per):
            rows = row_group(i)
            tiles = []
            for j in range(n_lane_tiles):
                if is_separable:
                    t = s_sc[c, rows, lane_tile(j)] - col_bias[:, lane_tile(j)]
                else:
                    d = (row + (i * rows_per + off)) - col[:, lane_tile(j)]
                    t = s_sc[c, rows, lane_tile(j)] - slope * jnp.abs(d.astype(F32))
                s_sc[c, rows, lane_tile(j)] = t
                tiles.append(t)
            pm_sc[c, rows, :] = _tree_reduce(jnp.maximum, tiles)

    def pass_b(c, first):
        for i in range(first, first + ATTN_UNIT // rows_per):
            rows = row_group(i)
            e = e_sc[c, rows, :]
            tiles = []
            for j in range(n_lane_tiles):
                p = jnp.exp2(s_sc[c, rows, lane_tile(j)] - e)
                p_sc[c, rows, lane_tile(j)] = p.astype(BF16)
                tiles.append(p)
            ps_sc[c, rows, :] = _tree_reduce(jnp.add, tiles)

    def tile(is_separable):
        v = v_ref[...]
        units = [(c, u) for u in range(tq // ATTN_UNIT) for c in range(2)]
        for c, u in units:
            rows = slice(u * ATTN_UNIT, (u + 1) * ATTN_UNIT)
            sl = slice(c * HEAD_DIM_A, (c + 1) * HEAD_DIM_A)
            s_sc[c, rows, :] = jnp.dot(q_ref[rows, sl], kt_ref[sl, :], preferred_element_type=F32)
        for c, u in units:
            rows = slice(u * ATTN_UNIT, (u + 1) * ATTN_UNIT)
            first = u * (ATTN_UNIT // rows_per)
            pass_a(c, first, is_separable)
            m_prev = m_sc[c, rows, :]
            m_new = jnp.maximum(m_prev, jnp.max(pm_sc[c, rows, :], axis=-1, keepdims=True) + shift[rows, :])
            alpha = jnp.exp2(m_prev - m_new)
            m_sc[c, rows, :] = m_new
            e_sc[c, rows, :] = jnp.broadcast_to(m_new - shift[rows, :], (ATTN_UNIT, LANES))
            pass_b(c, first)
            l_sc[c, rows, :] = alpha * l_sc[c, rows, :] + jnp.sum(ps_sc[c, rows, :], axis=-1, keepdims=True)
            acc_sc[c, rows, :] = (alpha * acc_sc[c, rows, :]
                                  + jnp.dot(p_sc[c, rows, :], v, preferred_element_type=F32))

    @pl.when(separable)
    def _():
        tile(True)

    @pl.when(jnp.logical_not(separable))
    def _():
        tile(False)

    @pl.when(ki == pl.num_programs(3) - 1)
    def _():
        o = acc_sc[0] / l_sc[0] - lam_ref[0] * (acc_sc[1] / l_sc[1])
        ms = jnp.mean(o * o, axis=-1, keepdims=True)
        o_ref[...] = (o * lax.rsqrt(ms + EPS) * g_ref[...]).astype(o_ref.dtype)


def _diff_attention(q, kt, vqkv, slopes, lam, subln_scaled, B, S, tq, tk):
    T = B * S
    nq, nk = S // tq, S // tk
    hw = 2 * HEAD_DIM_A
    assert tq % ATTN_UNIT == 0 and ATTN_UNIT % ATTN_ROWS == 0 and tk % LANES == 0
    return pl.pallas_call(
        functools.partial(_attn_kernel, tq=tq, tk=tk),
        grid=(B, N_HEADS_A, nq, nk),
        in_specs=[
            pl.BlockSpec(memory_space=pltpu.SMEM),
            pl.BlockSpec(memory_space=pltpu.SMEM),
            pl.BlockSpec((tq, hw), lambda b, h, i, j: (b * nq + i, h)),
            pl.BlockSpec((hw, tk), lambda b, h, i, j: (h, b * nk + j)),
            pl.BlockSpec((tk, hw), lambda b, h, i, j: (b * nk + j, h)),
            pl.BlockSpec((1, hw), lambda b, h, i, j: (0, 0)),
        ],
        out_specs=pl.BlockSpec((tq, hw), lambda b, h, i, j: (b * nq + i, h)),
        out_shape=jax.ShapeDtypeStruct((T, W_A), BF16),
        scratch_shapes=[pltpu.VMEM((2, tq, tk), F32), pltpu.VMEM((2, tq, tk), BF16),
                        pltpu.VMEM((2, tq, LANES), F32), pltpu.VMEM((2, tq, LANES), F32),
                        pltpu.VMEM((2, tq, LANES), F32),
                        pltpu.VMEM((2, tq, 1), F32), pltpu.VMEM((2, tq, 1), F32),
                        pltpu.VMEM((2, tq, hw), F32)],
        compiler_params=_params(("parallel", "parallel", "parallel", "arbitrary")),
        name="diff_attention",
    )(slopes, lam, q, kt, vqkv, subln_scaled.reshape(1, hw))


def _log_sigmoid(x):
    return jnp.minimum(x, 0.0) - jnp.log1p(jnp.exp(-jnp.abs(x)))


def _mlstm_chunk(q_ref, k_ref, v_ref, qk_cols, v_cols, gcol, grow, c_ref, n_ref, m_ref, h_ref, reverse):
    L = q_ref.shape[0]
    ii, fi = (2, 3) if reverse else (0, 1)
    r = lax.broadcasted_iota(jnp.int32, (L, L), 0)
    c = lax.broadcasted_iota(jnp.int32, (L, L), 1)
    lower = c <= r
    upper = c >= r
    tri_lo = jnp.where(lower, 1.0, 0.0).astype(F32)
    tri_up = jnp.where(upper, 1.0, 0.0).astype(F32)
    ic_row = grow[ii:ii + 1, :]
    ic_col = gcol[:, ii:ii + 1]
    lf_row = _log_sigmoid(grow[fi:fi + 1, :])
    lf_col = _log_sigmoid(gcol[:, fi:fi + 1])
    if reverse:
        b_row = jnp.dot(lf_row, tri_lo, precision=HIGHEST, preferred_element_type=F32)
        b_col = jnp.dot(tri_up, lf_col, precision=HIGHEST, preferred_element_type=F32)
        mask = upper
        b_last = b_row[:, 0:1]
    else:
        b_row = jnp.dot(lf_row, tri_up, precision=HIGHEST, preferred_element_type=F32)
        b_col = jnp.dot(tri_lo, lf_col, precision=HIGHEST, preferred_element_type=F32)
        mask = lower
        b_last = b_row[:, L - 1:L]
    qs = q_ref[:, qk_cols] * (DK_B ** -0.5)
    k = k_ref[:, qk_cols]
    s_raw = lax.dot_general(qs, k, (((1,), (1,)), ((), ())), preferred_element_type=F32)
    q_c = jnp.dot(qs, c_ref[...].astype(BF16), preferred_element_type=F32)
    yield
    m = m_ref[...]
    log_d = jnp.where(mask, b_col - b_row + ic_row, NEG)
    inter = b_col + m
    m_row = jnp.maximum(inter, jnp.max(log_d, axis=-1, keepdims=True))
    d_m = jnp.exp(log_d - m_row)
    sc = jnp.exp(inter - m_row)
    w_col = b_last - b_col + ic_col
    w_row = b_last - b_row + ic_row
    m_new = jnp.maximum(b_last + m, jnp.max(w_row, axis=-1, keepdims=True))
    decay = jnp.exp(b_last + m - m_new)
    ws_col = jnp.exp(w_col - m_new)
    s = s_raw * d_m
    v = v_ref[:, v_cols]
    wv = (ws_col * v.astype(F32)).astype(BF16)
    yield
    s_v = jnp.dot(s.astype(BF16), v, preferred_element_type=F32)
    k_wv = lax.dot_general(k, wv, (((0,), (0,)), ((), ())), preferred_element_type=F32)
    yield
    n_state = n_ref[...]
    num = s_v + sc * q_c
    den = (jnp.sum(s, axis=-1, keepdims=True)
           + sc * jnp.sum(qs.astype(F32) * n_state, axis=-1, keepdims=True))
    h_ref[:, v_cols] = num / jnp.maximum(jnp.abs(den), jnp.exp(-m_row))
    c_ref[...] = decay * c_ref[...] + k_wv
    n_ref[...] = decay * n_state + jnp.sum(ws_col * k.astype(F32), axis=0, keepdims=True)
    m_ref[...] = m_new


def _interleave(chains):
    chains = list(chains)
    while chains:
        alive = []
        for g in chains:
            try:
                next(g)
                alive.append(g)
            except StopIteration:
                pass
        chains = alive


def _mlstm_kernel(qf_ref, kf_ref, vf_ref, gcf_ref, grf_ref, qb_ref, kb_ref, vb_ref, gcb_ref, grb_ref,
                  hf_ref, hb_ref, c_sc, n_sc, m_sc):
    @pl.when(pl.program_id(1) == 0)
    def _():
        c_sc[...] = jnp.zeros(c_sc.shape, F32)
        n_sc[...] = jnp.zeros(n_sc.shape, F32)
        m_sc[...] = jnp.full(m_sc.shape, NEG, F32)

    chains = []
    for h in range(N_HEADS_B):
        qk = slice(h * DK_B, (h + 1) * DK_B)
        vv = slice(h * DV_B, (h + 1) * DV_B)
        chains.append(_mlstm_chunk(qf_ref, kf_ref, vf_ref, qk, vv, gcf_ref[h], grf_ref[h],
                                   c_sc.at[0, h], n_sc.at[0, h], m_sc.at[0, h], hf_ref, reverse=False))
        chains.append(_mlstm_chunk(qb_ref, kb_ref, vb_ref, qk, vv, gcb_ref[h], grb_ref[h],
                                   c_sc.at[1, h], n_sc.at[1, h], m_sc.at[1, h], hb_ref, reverse=True))
    _interleave(chains)


def _mlstm(vqkv, gcol, grow, B, S):
    T = B * S
    nc = S // CHUNK
    q_off = W_A // QK_B
    k_off = (W_A + QK_B) // QK_B
    v_off = (W_A + 2 * QK_B) // W_B

    def fwd(b, c):
        return b * nc + c

    def bwd(b, c):
        return b * nc + nc - 1 - c

    def specs(pos):
        return [
            pl.BlockSpec((CHUNK, QK_B), lambda b, c: (pos(b, c), q_off)),
            pl.BlockSpec((CHUNK, QK_B), lambda b, c: (pos(b, c), k_off)),
            pl.BlockSpec((CHUNK, W_B), lambda b, c: (pos(b, c), v_off)),
            pl.BlockSpec((None, N_HEADS_B, CHUNK, 4), lambda b, c: (b, 0, pos(b, c) - b * nc, 0)),
            pl.BlockSpec((None, N_HEADS_B, 4, CHUNK), lambda b, c: (b, 0, 0, pos(b, c) - b * nc)),
        ]

    out_sds = jax.ShapeDtypeStruct((T, W_B), F32)
    return pl.pallas_call(
        _mlstm_kernel,
        grid=(B, nc),
        in_specs=specs(fwd) + specs(bwd),
        out_specs=[pl.BlockSpec((CHUNK, W_B), lambda b, c: (fwd(b, c), 0)),
                   pl.BlockSpec((CHUNK, W_B), lambda b, c: (bwd(b, c), 0))],
        out_shape=[out_sds, out_sds],
        scratch_shapes=[pltpu.VMEM((2, N_HEADS_B, DK_B, DV_B), F32), pltpu.VMEM((2, N_HEADS_B, 1, DK_B), F32),
                        pltpu.VMEM((2, N_HEADS_B, 1, 1), F32)],
        compiler_params=_params(("parallel", "arbitrary")),
        name="mlstm",
    )(vqkv, vqkv, vqkv, gcol, grow, vqkv, vqkv, vqkv, gcol, grow)


def _mlstm_out_kernel(hf_ref, hb_ref, o_ref, g_ref, y_ref):
    for j in range(N_HEADS_B):
        sl = slice(j * DV_B, (j + 1) * DV_B)
        h = hf_ref[:, sl] + hb_ref[:, sl]
        ms = jnp.mean(h * h, axis=-1, keepdims=True)
        hn = h * lax.rsqrt(ms + EPS) * g_ref[...]
        y_ref[:, sl] = (jax.nn.sigmoid(o_ref[:, sl]) * hn).astype(y_ref.dtype)


def _mlstm_out(hf, hb, om, norm_g, tm):
    T = hf.shape[0]
    row = pl.BlockSpec((tm, W_B), lambda i: (i, 0))
    return pl.pallas_call(
        _mlstm_out_kernel,
        grid=(T // tm,),
        in_specs=[row, row, row, pl.BlockSpec((1, DV_B), lambda i: (0, 0))],
        out_specs=row,
        out_shape=jax.ShapeDtypeStruct((T, W_B), BF16),
        compiler_params=_params(("parallel",)),
        name="mlstm_out",
    )(hf, hb, om, norm_g.reshape(1, DV_B))


def _merge_kernel(ya_ref, yb_ref, wa_ref, wb_ref, ga_ref, gb_ref, o_ref):
    pa = jnp.dot(ya_ref[...], wa_ref[...], preferred_element_type=F32)
    pb = jnp.dot(yb_ref[...], wb_ref[...], preferred_element_type=F32)
    o_ref[...] = (jax.nn.sigmoid(ga_ref[...]) * pa + jax.nn.sigmoid(gb_ref[...]) * pb).astype(o_ref.dtype)


def _merge(ya, yb, wa, wb, gates, tm, tn):
    T = ya.shape[0]
    nb = D_MODEL // tn
    return pl.pallas_call(
        _merge_kernel,
        grid=(T // tm, nb),
        in_specs=[pl.BlockSpec((tm, W_A), lambda i, j: (i, 0)),
                  pl.BlockSpec((tm, W_B), lambda i, j: (i, 0)),
                  pl.BlockSpec((W_A, tn), lambda i, j: (0, j)),
                  pl.BlockSpec((W_B, tn), lambda i, j: (0, j)),
                  pl.BlockSpec((tm, tn), lambda i, j: (i, j)),
                  pl.BlockSpec((tm, tn), lambda i, j: (i, nb + j))],
        out_specs=pl.BlockSpec((tm, tn), lambda i, j: (i, j)),
        out_shape=jax.ShapeDtypeStruct((T, D_MODEL), BF16),
        compiler_params=_params(("parallel", "arbitrary")),
        name="merge",
    )(ya, yb, wa, wb, gates, gates)


def _outproj_kernel(a_ref, w_ref, x_ref, o_ref):
    o_ref[...] = x_ref[...] + jnp.dot(a_ref[...], w_ref[...], preferred_element_type=F32)


def _outproj(merged, w, x, tm, tn):
    T = merged.shape[0]
    return pl.pallas_call(
        _outproj_kernel,
        grid=(T // tm, D_MODEL // tn),
        in_specs=[pl.BlockSpec((tm, D_MODEL), lambda i, j: (i, 0)),
                  pl.BlockSpec((D_MODEL, tn), lambda i, j: (0, j)),
                  pl.BlockSpec((tm, tn), lambda i, j: (i, j))],
        out_specs=pl.BlockSpec((tm, tn), lambda i, j: (i, j)),
        out_shape=jax.ShapeDtypeStruct((T, D_MODEL), F32),
        compiler_params=_params(("parallel", "arbitrary")),
        name="outproj",
    )(merged, w, x)


def _router_kernel(x_ref, g_ref, wr_ref, br_ref, xn_ref, r_ref):
    x = x_ref[...]
    ms = jnp.mean(x * x, axis=-1, keepdims=True)
    xn = x * lax.rsqrt(ms + EPS) * g_ref[...]
    xn_ref[...] = xn
    logits = jnp.dot(xn, wr_ref[...], precision=HIGHEST, preferred_element_type=F32) + br_ref[...]
    lane = lax.broadcasted_iota(jnp.int32, logits.shape, 1)
    cm = jnp.where(lane < N_GROUPS, logits, NEG)
    cmax = jnp.max(cm, axis=-1, keepdims=True)
    p_top = 1.0 / jnp.sum(jnp.exp(cm - cmax), axis=-1, keepdims=True)
    grp = jnp.min(jnp.where(cm == cmax, lane, LANES), axis=-1, keepdims=True)
    lo = N_GROUPS + EXP_PER_GROUP * grp
    fm = jnp.where((lane >= lo) & (lane < lo + EXP_PER_GROUP), logits, NEG)
    fmax = jnp.max(fm, axis=-1, keepdims=True)
    fe = jnp.exp(fm - fmax)
    pe = fe / jnp.sum(fe, axis=-1, keepdims=True)
    i1 = jnp.min(jnp.where(fm == fmax, lane, LANES), axis=-1, keepdims=True)
    p1 = jnp.sum(jnp.where(lane == i1, pe, 0.0), axis=-1, keepdims=True)
    fm2 = jnp.where(lane == i1, NEG, fm)
    f2max = jnp.max(fm2, axis=-1, keepdims=True)
    i2 = jnp.min(jnp.where(fm2 == f2max, lane, LANES), axis=-1, keepdims=True)
    p2 = jnp.sum(jnp.where(lane == i2, pe, 0.0), axis=-1, keepdims=True)
    g1 = p_top * (p1 / (p1 + p2))
    g2 = p_top * (p2 / (p1 + p2))
    e1 = (i1 - N_GROUPS).astype(F32)
    e2 = (i2 - N_GROUPS).astype(F32)
    r_ref[...] = jnp.where(lane == 0, e1, jnp.where(lane == 1, e2, jnp.where(lane == 2, g1,
                           jnp.where(lane == 3, g2, 0.0))))


def _router(x1, g, wr, br, tm):
    T = x1.shape[0]
    return pl.pallas_call(
        _router_kernel,
        grid=(T // tm,),
        in_specs=[pl.BlockSpec((tm, D_MODEL), lambda i: (i, 0)),
                  pl.BlockSpec((1, D_MODEL), lambda i: (0, 0)),
                  pl.BlockSpec((D_MODEL, LANES), lambda i: (0, 0)),
                  pl.BlockSpec((1, LANES), lambda i: (0, 0))],
        out_specs=[pl.BlockSpec((tm, D_MODEL), lambda i: (i, 0)),
                   pl.BlockSpec((tm, LANES), lambda i: (i, 0))],
        out_shape=[jax.ShapeDtypeStruct((T, D_MODEL), F32), jax.ShapeDtypeStruct((T, LANES), F32)],
        compiler_params=_params(("parallel",)),
        name="router",
    )(x1, g.reshape(1, D_MODEL), wr, br)


def _row_copy(src_hbm, row, dst, r, sem):
    return pltpu.make_async_copy(src_hbm.at[pl.ds(row, 1)], dst.at[pl.ds(r, 1)], sem)


def _moe_kernel(tok_ref, be_ref, nused_ref, x_hbm, w1_ref, w3_ref, w2_ref, o_ref, xbuf, sems):
    b = pl.program_id(0)
    nused = nused_ref[0]
    cur = lax.rem(b, 2)

    def gather(blk, buf, fn):
        def body(r, carry):
            fn(_row_copy(x_hbm, tok_ref[blk * MOE_ROWS + r], xbuf.at[buf], r, sems.at[buf]))
            return carry

        lax.fori_loop(0, MOE_ROWS, body, 0, unroll=8)

    @pl.when(b == 0)
    def _():
        gather(0, 0, lambda cp: cp.start())

    @pl.when(b + 1 < nused)
    def _():
        gather(b + 1, 1 - cur, lambda cp: cp.start())

    @pl.when(b < nused)
    def _():
        gather(b, cur, lambda cp: cp.wait())
        x = xbuf[cur].astype(BF16)
        h1 = jnp.dot(x, w1_ref[...], preferred_element_type=F32)
        h3 = jnp.dot(x, w3_ref[...], preferred_element_type=F32)
        hid = (h1 * jax.nn.sigmoid(h1)) * h3
        o_ref[...] = jnp.dot(hid.astype(BF16), w2_ref[...], preferred_element_type=F32)

    @pl.when(b >= nused)
    def _():
        o_ref[...] = jnp.zeros(o_ref.shape, o_ref.dtype)


def _moe_mlp(xn, slot_tok, w1, w3, w2, blk_exp, nused):
    ns = slot_tok.shape[0]
    nb = ns // MOE_ROWS
    once = pl.Buffered(1)
    return pl.pallas_call(
        _moe_kernel,
        grid_spec=pltpu.PrefetchScalarGridSpec(
            num_scalar_prefetch=3,
            grid=(nb,),
            in_specs=[
                pl.BlockSpec(memory_space=pl.ANY),
                pl.BlockSpec((None, D_MODEL, D_FF_EXP), lambda b, tok, be, nu: (be[b], 0, 0), pipeline_mode=once),
                pl.BlockSpec((None, D_MODEL, D_FF_EXP), lambda b, tok, be, nu: (be[b], 0, 0), pipeline_mode=once),
                pl.BlockSpec((None, D_FF_EXP, D_MODEL), lambda b, tok, be, nu: (be[b], 0, 0), pipeline_mode=once),
            ],
            out_specs=pl.BlockSpec((MOE_ROWS, D_MODEL), lambda b, tok, be, nu: (b, 0)),
            scratch_shapes=[pltpu.VMEM((2, MOE_ROWS, D_MODEL), F32), pltpu.SemaphoreType.DMA((2,))],
        ),
        out_shape=jax.ShapeDtypeStruct((ns, D_MODEL), F32),
        compiler_params=_params(("arbitrary",)),
        name="moe_mlp",
    )(slot_tok, blk_exp, nused, xn, w1, w3, w2)


def _combine_kernel(pos_ref, x_ref, r_ref, o_hbm, y_ref, buf, sems, *, rows):
    i = pl.program_id(0)
    cur = lax.rem(i, 2)

    def gather(step, slot, fn):
        def body(r, carry):
            for k in range(2):
                fn(_row_copy(o_hbm, pos_ref[2 * (step * rows + r) + k], buf.at[slot, k], r, sems.at[slot]))
            return carry

        lax.fori_loop(0, rows, body, 0, unroll=4)

    @pl.when(i == 0)
    def _():
        gather(0, 0, lambda cp: cp.start())

    @pl.when(i + 1 < pl.num_programs(0))
    def _():
        gather(i + 1, 1 - cur, lambda cp: cp.start())

    gather(i, cur, lambda cp: cp.wait())
    r = r_ref[...]
    y_ref[...] = x_ref[...] + (buf[cur, 0] * r[:, 2:3] + buf[cur, 1] * r[:, 3:4])


def _combine(x1, route, o, pos, rows):
    T, D = x1.shape
    return pl.pallas_call(
        functools.partial(_combine_kernel, rows=rows),
        grid_spec=pltpu.PrefetchScalarGridSpec(
            num_scalar_prefetch=1,
            grid=(T // rows,),
            in_specs=[pl.BlockSpec((rows, D), lambda i, pos: (i, 0)),
                      pl.BlockSpec((rows, LANES), lambda i, pos: (i, 0)),
                      pl.BlockSpec(memory_space=pl.ANY)],
            out_specs=pl.BlockSpec((rows, D), lambda i, pos: (i, 0)),
            scratch_shapes=[pltpu.VMEM((2, 2, rows, D), F32), pltpu.SemaphoreType.DMA((2,))],
        ),
        out_shape=jax.ShapeDtypeStruct((T, D), F32),
        compiler_params=_params(("arbitrary",)),
        name="moe_combine",
    )(pos, x1, route, o)


def _dispatch_plan(route, T):
    i32 = jnp.int32
    tk = 2 * T
    nb = tk // MOE_ROWS + N_EXP
    experts = jnp.arange(N_EXP, dtype=i32)
    e_flat = route[:, 0:2].astype(i32).reshape(-1)
    order = jnp.argsort(e_flat).astype(i32)
    rank = jnp.argsort(order).astype(i32)
    counts = jnp.sum(e_flat[:, None] == experts[None, :], axis=0, dtype=i32)
    pcounts = (counts + MOE_ROWS - 1) // MOE_ROWS * MOE_ROWS
    pends = jnp.cumsum(pcounts)
    pstarts = pends - pcounts
    ustarts = jnp.cumsum(counts) - counts
    pos = pstarts[e_flat] + rank - ustarts[e_flat]
    slot = jnp.arange(nb * MOE_ROWS, dtype=i32)
    slot_exp = jnp.minimum(jnp.sum(pends[None, :] <= slot[:, None], axis=1, dtype=i32), N_EXP - 1)
    within = slot - pstarts[slot_exp]
    src = jnp.clip(ustarts[slot_exp] + within, 0, tk - 1)
    slot_tok = jnp.where(within < counts[slot_exp], order[src] // 2, 0).astype(i32)
    blk_exp = slot_exp[::MOE_ROWS]
    nused = (pends[-1:] // MOE_ROWS).astype(i32)
    return slot_tok, blk_exp, nused, pos.astype(i32)


def _tile(n, pref):
    return pref if n % pref == 0 else n


def _trunk(x, p):
    B, S, D = x.shape
    T = B * S
    xt = x.reshape(T, D)
    tm = _tile(T, PROJ_TM)
    tn = PROJ_TN
    tr = _tile(T, ROW_TM)

    xn = _rmsnorm(xt, p["norm_mix_g"], tr)
    q = _proj(xn, p["w_main"], W_A, BF16, tm, tn, col_off=0, vec=p["q_gain"], kind="headnorm", name="proj_q")
    kt = _proj(xn, p["w_main"], W_A, BF16, tm, tn, col_off=W_A, vec=p["k_gain"], kind="headnorm_t",
               name="proj_kt")
    vqkv = _proj(xn, p["w_main"], W_A + 2 * QK_B + W_B, BF16, tm, tn, col_off=2 * W_A, name="proj_vqkv")
    om = _proj(xn, p["w_main"], W_B, F32, tm, tn, col_off=3 * W_A + 2 * QK_B + W_B, name="proj_o")
    gates = _proj(xn, p["w_gate"], 2 * D_MODEL, F32, tm, tn, name="proj_gates")
    gif = _proj(xn, p["w_gif"], LANES, F32, tm, LANES, vec=p["b_gif"], kind="bias", name="proj_gif")

    ya = _diff_attention(q, kt, vqkv, p["slopes"], p["lam"], p["subln_scaled"], B, S,
                         _tile(S, ATTN_TQ), _tile(S, ATTN_TK))

    g4 = gif[:, :4 * N_HEADS_B].reshape(B, S, 4, N_HEADS_B)
    gcol = g4.transpose(0, 3, 1, 2)
    grow = g4.transpose(0, 3, 2, 1)
    hf, hb = _mlstm(vqkv, gcol, grow, B, S)
    yb = _mlstm_out(hf, hb, om, p["mlstm_norm_g"], tr)

    merged = _merge(ya, yb, p["w_branch_a"], p["w_branch_b"], gates, tm, MERGE_TN)
    x1 = _outproj(merged, p["w_out"], xt, tm, MERGE_TN)

    xn2, route = _router(x1, p["norm_ffn_g"], p["w_router"], p["b_router"], tr)
    slot_tok, blk_exp, nused, pos = _dispatch_plan(route, T)
    o = _moe_mlp(xn2, slot_tok, p["w1"], p["w3"], p["w2"], blk_exp, nused)
    y = _combine(x1, route, o, pos, _tile(T, 128))
    return y.reshape(B, S, D)


def _prepare(norm_mix_g, w_in, b_mgate, attn_q_norm_g, attn_k_norm_g, lambda_q1, lambda_k1, lambda_q2,
             lambda_k2, attn_subln_g, mlstm_norm_g, w_branch_a, w_branch_b, w_out, norm_ffn_g,
             w_router_group, b_router_group, w_router_expert, b_router_expert, w_exp_gate, w_exp_up,
             w_exp_down):
    l = 0
    n_main = 3 * W_A + 2 * QK_B + 2 * W_B
    n_if = 4 * N_HEADS_B
    w = w_in[l]
    q_gain = jnp.tile((attn_q_norm_g[l].astype(F32) * (HEAD_DIM_A ** -0.5 * LOG2E)).reshape(-1), N_HEADS_A)
    k_gain = jnp.tile(attn_k_norm_g[l].astype(F32).reshape(-1), N_HEADS_A)
    lam = (jnp.exp(jnp.sum(lambda_q1[l].astype(F32) * lambda_k1[l].astype(F32)))
           - jnp.exp(jnp.sum(lambda_q2[l].astype(F32) * lambda_k2[l].astype(F32))) + LAM_INIT)
    slopes = LOG2E * 2.0 ** (-8.0 * (jnp.arange(N_HEADS_A, dtype=F32) + 1.0) / N_HEADS_A)
    w_router = jnp.zeros((D_MODEL, LANES), F32)
    w_router = w_router.at[:, :N_GROUPS].set(w_router_group[l])
    w_router = w_router.at[:, N_GROUPS:N_GROUPS + N_EXP].set(w_router_expert[l])
    b_router = jnp.zeros((1, LANES), F32)
    b_router = b_router.at[0, :N_GROUPS].set(b_router_group[l].astype(F32))
    b_router = b_router.at[0, N_GROUPS:N_GROUPS + N_EXP].set(b_router_expert[l].astype(F32).reshape(-1))
    return {
        "norm_mix_g": norm_mix_g[l],
        "w_main": w[:, :n_main].astype(BF16),
        "w_gate": w[:, n_main + n_if:].astype(BF16),
        "w_gif": jnp.pad(w[:, n_main:n_main + n_if], ((0, 0), (0, LANES - n_if))).astype(BF16),
        "b_gif": jnp.pad(b_mgate[l].astype(F32), (0, LANES - n_if)),
        "q_gain": q_gain,
        "k_gain": k_gain,
        "lam": lam.reshape(1),
        "slopes": slopes,
        "subln_scaled": attn_subln_g[l].astype(F32) * (1.0 - LAM_INIT),
        "mlstm_norm_g": mlstm_norm_g[l],
        "w_branch_a": w_branch_a[l].astype(BF16),
        "w_branch_b": w_branch_b[l].astype(BF16),
        "w_out": w_out[l].astype(BF16),
        "norm_ffn_g": norm_ffn_g[l],
        "w_router": w_router,
        "b_router": b_router,
        "w1": w_exp_gate[l].astype(BF16),
        "w3": w_exp_up[l].astype(BF16),
        "w2": w_exp_down[l].astype(BF16),
    }


def kernel(x_prompt, x_sample, norm_mix_g, w_in, b_mgate, attn_q_norm_g, attn_k_norm_g, lambda_q1, lambda_k1, lambda_q2, lambda_k2, attn_subln_g, mlstm_norm_g, w_branch_a, w_branch_b, w_out, norm_ffn_g, w_router_group, b_router_group, w_router_expert, b_router_expert, w_exp_gate, w_exp_up, w_exp_down):
    p = _prepare(norm_mix_g, w_in, b_mgate, attn_q_norm_g, attn_k_norm_g, lambda_q1, lambda_k1, lambda_q2,
                 lambda_k2, attn_subln_g, mlstm_norm_g, w_branch_a, w_branch_b, w_out, norm_ffn_g,
                 w_router_group, b_router_group, w_router_expert, b_router_expert, w_exp_gate, w_exp_up,
                 w_exp_down)
    return (_trunk(x_prompt, p), _trunk(x_sample, p))
```

```python
import functools

import jax
import jax.numpy as jnp
from jax import lax
from jax.experimental import pallas as pl
from jax.experimental.pallas import tpu as pltpu

D_MODEL = 4096
HEAD_DIM_A = 128
N_HEADS_A = 8
W_A = 2048
N_HEADS_B = 4
DV_B = 512
DK_B = 256
W_B = 2048
QK_B = 1024
CHUNK = 128
N_GROUPS = 4
EXP_PER_GROUP = 8
N_EXP = 32
D_FF_EXP = 1024
EPS = 1e-6
NEG = -1e30
LAM_INIT = 0.8 - 0.6 * 1.0

LANES = 128
MOE_ROWS = 256
PROJ_TM, PROJ_TN = 1024, 1024
MERGE_TN = 512
ROW_TM = 256
ATTN_TQ, ATTN_TK = 512, 2048
ATTN_UNIT = 256
VMEM_LIMIT = 56 * 1024 * 1024
ATTN_ROWS = 16
LOG2E = 1.4426950408889634

F32 = jnp.float32
BF16 = jnp.bfloat16
HIGHEST = lax.Precision.HIGHEST


def _params(sem, vmem=VMEM_LIMIT):
    return pltpu.CompilerParams(dimension_semantics=sem, vmem_limit_bytes=vmem)


def _rmsnorm_kernel(x_ref, g_ref, o_ref):
    x = x_ref[...]
    ms = jnp.mean(x * x, axis=-1, keepdims=True)
    o_ref[...] = (x * lax.rsqrt(ms + EPS) * g_ref[...]).astype(o_ref.dtype)


def _rmsnorm(x, g, tm):
    T, D = x.shape
    return pl.pallas_call(
        _rmsnorm_kernel,
        grid=(T // tm,),
        in_specs=[pl.BlockSpec((tm, D), lambda i: (i, 0)), pl.BlockSpec((1, D), lambda i: (0, 0))],
        out_specs=pl.BlockSpec((tm, D), lambda i: (i, 0)),
        out_shape=jax.ShapeDtypeStruct((T, D), BF16),
        compiler_params=_params(("parallel",)),
        name="rmsnorm",
    )(x, g.reshape(1, D))


def _proj_kernel(a_ref, w_ref, o_ref):
    o_ref[...] = jnp.dot(a_ref[...], w_ref[...], preferred_element_type=F32).astype(o_ref.dtype)


def _proj_bias_kernel(a_ref, w_ref, b_ref, o_ref):
    acc = jnp.dot(a_ref[...], w_ref[...], preferred_element_type=F32)
    o_ref[...] = (acc + b_ref[...]).astype(o_ref.dtype)


def _proj_headnorm_kernel(a_ref, w_ref, g_ref, o_ref, *, transposed):
    acc = jnp.dot(a_ref[...], w_ref[...], preferred_element_type=F32)
    for j in range(acc.shape[1] // HEAD_DIM_A):
        sl = slice(j * HEAD_DIM_A, (j + 1) * HEAD_DIM_A)
        blk = acc[:, sl]
        ms = jnp.mean(blk * blk, axis=-1, keepdims=True)
        y = blk * lax.rsqrt(ms + EPS) * g_ref[:, sl]
        if transposed:
            o_ref[sl, :] = y.T.astype(o_ref.dtype)
        else:
            o_ref[:, sl] = y.astype(o_ref.dtype)


def _proj(a, w, n_out, out_dtype, tm, tn, *, col_off=0, vec=None, kind="plain", name="proj"):
    T, K = a.shape
    assert n_out % tn == 0 and col_off % tn == 0 and T % tm == 0
    off = col_off // tn
    in_specs = [pl.BlockSpec((tm, K), lambda i, j: (i, 0)),
                pl.BlockSpec((K, tn), lambda i, j: (0, j + off))]
    args = [a, w]
    if kind == "plain":
        body = _proj_kernel
    else:
        body = (_proj_bias_kernel if kind == "bias" else
                functools.partial(_proj_headnorm_kernel, transposed=(kind == "headnorm_t")))
        in_specs.append(pl.BlockSpec((1, tn), lambda i, j: (0, j)))
        args.append(vec.reshape(1, n_out).astype(F32))
    if kind == "headnorm_t":
        out_spec = pl.BlockSpec((tn, tm), lambda i, j: (j, i))
        out_shape = jax.ShapeDtypeStruct((n_out, T), out_dtype)
    else:
        out_spec = pl.BlockSpec((tm, tn), lambda i, j: (i, j))
        out_shape = jax.ShapeDtypeStruct((T, n_out), out_dtype)
    return pl.pallas_call(
        body,
        grid=(T // tm, n_out // tn),
        in_specs=in_specs,
        out_specs=out_spec,
        out_shape=out_shape,
        compiler_params=_params(("parallel", "arbitrary")),
        name=name,
    )(*args)


def _tree_reduce(op, xs):
    while len(xs) > 1:
        xs = [op(xs[k], xs[k + 1]) if k + 1 < len(xs) else xs[k] for k in range(0, len(xs), 2)]
    return xs[0]


def _attn_kernel(slopes_ref, lam_ref, q_ref, kt_ref, v_ref, g_ref, o_ref,
                 s_sc, p_sc, pm_sc, ps_sc, e_sc, m_sc, l_sc, acc_sc, *, tq, tk):
    h = pl.program_id(1)
    qi = pl.program_id(2)
    ki = pl.program_id(3)
    rows_per = ATTN_ROWS
    n_lane_tiles = tk // LANES

    @pl.when(ki == 0)
    def _():
        m_sc[...] = jnp.full(m_sc.shape, NEG, F32)
        l_sc[...] = jnp.zeros(l_sc.shape, F32)
        acc_sc[...] = jnp.zeros(acc_sc.shape, F32)

    slope = slopes_ref[h]
    off = qi * tq - ki * tk
    left = off >= tk - 1
    right = off <= 1 - tq
    separable = left | right
    col = lax.broadcasted_iota(jnp.int32, (1, tk), 1)
    row = lax.broadcasted_iota(jnp.int32, (rows_per, 1), 0)
    all_rows = lax.broadcasted_iota(jnp.int32, (tq, 1), 0)
    col_bias = slope * jnp.where(left, tk - 1 - col, col).astype(F32)
    row_term = jnp.where(left, all_rows + (off - (tk - 1)), -off - all_rows).astype(F32)
    shift = jnp.where(separable, -slope * row_term, 0.0)

    def lane_tile(j):
        return slice(j * LANES, (j + 1) * LANES)

    def row_group(i):
        return slice(i * rows_per, (i + 1) * rows_per)

    def pass_a(c, first, is_separable):
        for i in range(first, first + ATTN_UNIT // rows_per):
            rows = row_group(i)
            tiles = []
            for j in range(n_lane_tiles):
                if is_separable:
                    t = s_sc[c, rows, lane_tile(j)] - col_bias[:, lane_tile(j)]
                else:
                    d = (row + (i * rows_per + off)) - col[:, lane_tile(j)]
                    t = s_sc[c, rows, lane_tile(j)] - slope * jnp.abs(d.astype(F32))
                s_sc[c, rows, lane_tile(j)] = t
                tiles.append(t)
            pm_sc[c, rows, :] = _tree_reduce(jnp.maximum, tiles)

    def pass_b(c, first):
        for i in range(first, first + ATTN_UNIT // rows_per):
            rows = row_group(i)
            e = e_sc[c, rows, :]
            tiles = []
            for j in range(n_lane_tiles):
                p = jnp.exp2(s_sc[c, rows, lane_tile(j)] - e)
                p_sc[c, rows, lane_tile(j)] = p.astype(BF16)
                tiles.append(p)
            ps_sc[c, rows, :] = _tree_reduce(jnp.add, tiles)

    def tile(is_separable):
        v = v_ref[...]
        units = [(c, u) for u in range(tq // ATTN_UNIT) for c in range(2)]
        for c, u in units:
            rows = slice(u * ATTN_UNIT, (u + 1) * ATTN_UNIT)
            sl = slice(c * HEAD_DIM_A, (c + 1) * HEAD_DIM_A)
            s_sc[c, rows, :] = jnp.dot(q_ref[rows, sl], kt_ref[sl, :], preferred_element_type=F32)
        for c, u in units:
            rows = slice(u * ATTN_UNIT, (u + 1) * ATTN_UNIT)
            first = u * (ATTN_UNIT // rows_per)
            pass_a(c, first, is_separable)
            m_prev = m_sc[c, rows, :]
            m_new = jnp.maximum(m_prev, jnp.max(pm_sc[c, rows, :], axis=-1, keepdims=True) + shift[rows, :])
            alpha = jnp.exp2(m_prev - m_new)
            m_sc[c, rows, :] = m_new
            e_sc[c, rows, :] = jnp.broadcast_to(m_new - shift[rows, :], (ATTN_UNIT, LANES))
            pass_b(c, first)
            l_sc[c, rows, :] = alpha * l_sc[c, rows, :] + jnp.sum(ps_sc[c, rows, :], axis=-1, keepdims=True)
            acc_sc[c, rows, :] = (alpha * acc_sc[c, rows, :]
                                  + jnp.dot(p_sc[c, rows, :], v, preferred_element_type=F32))

    @pl.when(separable)
    def _():
        tile(True)

    @pl.when(jnp.logical_not(separable))
    def _():
        tile(False)

    @pl.when(ki == pl.num_programs(3) - 1)
    def _():
        o = acc_sc[0] / l_sc[0] - lam_ref[0] * (acc_sc[1] / l_sc[1])
        ms = jnp.mean(o * o, axis=-1, keepdims=True)
        o_ref[...] = (o * lax.rsqrt(ms + EPS) * g_ref[...]).astype(o_ref.dtype)


def _diff_attention(q, kt, vqkv, slopes, lam, subln_scaled, B, S, tq, tk):
    T = B * S
    nq, nk = S // tq, S // tk
    hw = 2 * HEAD_DIM_A
    assert tq % ATTN_UNIT == 0 and ATTN_UNIT % ATTN_ROWS == 0 and tk % LANES == 0
    return pl.pallas_call(
        functools.partial(_attn_kernel, tq=tq, tk=tk),
        grid=(B, N_HEADS_A, nq, nk),
        in_specs=[
            pl.BlockSpec(memory_space=pltpu.SMEM),
            pl.BlockSpec(memory_space=pltpu.SMEM),
            pl.BlockSpec((tq, hw), lambda b, h, i, j: (b * nq + i, h)),
            pl.BlockSpec((hw, tk), lambda b, h, i, j: (h, b * nk + j)),
            pl.BlockSpec((tk, hw), lambda b, h, i, j: (b * nk + j, h)),
            pl.BlockSpec((1, hw), lambda b, h, i, j: (0, 0)),
        ],
        out_specs=pl.BlockSpec((tq, hw), lambda b, h, i, j: (b * nq + i, h)),
        out_shape=jax.ShapeDtypeStruct((T, W_A), BF16),
        scratch_shapes=[pltpu.VMEM((2, tq, tk), F32), pltpu.VMEM((2, tq, tk), BF16),
                        pltpu.VMEM((2, tq, LANES), F32), pltpu.VMEM((2, tq, LANES), F32),
                        pltpu.VMEM((2, tq, LANES), F32),
                        pltpu.VMEM((2, tq, 1), F32), pltpu.VMEM((2, tq, 1), F32),
                        pltpu.VMEM((2, tq, hw), F32)],
        compiler_params=_params(("parallel", "parallel", "parallel", "arbitrary")),
        name="diff_attention",
    )(slopes, lam, q, kt, vqkv, subln_scaled.reshape(1, hw))


def _log_sigmoid(x):
    return jnp.minimum(x, 0.0) - jnp.log1p(jnp.exp(-jnp.abs(x)))


def _mlstm_chunk(q_ref, k_ref, v_ref, qk_cols, v_cols, gcol, grow, c_ref, n_ref, m_ref, h_ref, reverse):
    L = q_ref.shape[0]
    ii, fi = (2, 3) if reverse else (0, 1)
    r = lax.broadcasted_iota(jnp.int32, (L, L), 0)
    c = lax.broadcasted_iota(jnp.int32, (L, L), 1)
    lower = c <= r
    upper = c >= r
    tri_lo = jnp.where(lower, 1.0, 0.0).astype(F32)
    tri_up = jnp.where(upper, 1.0, 0.0).astype(F32)
    ic_row = grow[ii:ii + 1, :]
    ic_col = gcol[:, ii:ii + 1]
    lf_row = _log_sigmoid(grow[fi:fi + 1, :])
    lf_col = _log_sigmoid(gcol[:, fi:fi + 1])
    if reverse:
        b_row = jnp.dot(lf_row, tri_lo, precision=HIGHEST, preferred_element_type=F32)
        b_col = jnp.dot(tri_up, lf_col, precision=HIGHEST, preferred_element_type=F32)
        mask = upper
        b_last = b_row[:, 0:1]
    else:
        b_row = jnp.dot(lf_row, tri_up, precision=HIGHEST, preferred_element_type=F32)
        b_col = jnp.dot(tri_lo, lf_col, precision=HIGHEST, preferred_element_type=F32)
        mask = lower
        b_last = b_row[:, L - 1:L]
    qs = q_ref[:, qk_cols] * (DK_B ** -0.5)
    k = k_ref[:, qk_cols]
    s_raw = lax.dot_general(qs, k, (((1,), (1,)), ((), ())), preferred_element_type=F32)
    q_c = jnp.dot(qs, c_ref[...].astype(BF16), preferred_element_type=F32)
    yield
    m = m_ref[...]
    log_d = jnp.where(mask, b_col - b_row + ic_row, NEG)
    inter = b_col + m
    m_row = jnp.maximum(inter, jnp.max(log_d, axis=-1, keepdims=True))
    d_m = jnp.exp(log_d - m_row)
    sc = jnp.exp(inter - m_row)
    w_col = b_last - b_col + ic_col
    w_row = b_last - b_row + ic_row
    m_new = jnp.maximum(b_last + m, jnp.max(w_row, axis=-1, keepdims=True))
    decay = jnp.exp(b_last + m - m_new)
    ws_col = jnp.exp(w_col - m_new)
    s = s_raw * d_m
    v = v_ref[:, v_cols]
    wv = (ws_col * v.astype(F32)).astype(BF16)
    yield
    s_v = jnp.dot(s.astype(BF16), v, preferred_element_type=F32)
    k_wv = lax.dot_general(k, wv, (((0,), (0,)), ((), ())), preferred_element_type=F32)
    yield
    n_state = n_ref[...]
    num = s_v + sc * q_c
    den = (jnp.sum(s, axis=-1, keepdims=True)
           + sc * jnp.sum(qs.astype(F32) * n_state, axis=-1, keepdims=True))
    h_ref[:, v_cols] = num / jnp.maximum(jnp.abs(den), jnp.exp(-m_row))
    c_ref[...] = decay * c_ref[...] + k_wv
    n_ref[...] = decay * n_state + jnp.sum(ws_col * k.astype(F32), axis=0, keepdims=True)
    m_ref[...] = m_new


def _interleave(chains):
    chains = list(chains)
    while chains:
        alive = []
        for g in chains:
            try:
                next(g)
                alive.append(g)
            except StopIteration:
                pass
        chains = alive


def _mlstm_kernel(qf_ref, kf_ref, vf_ref, gcf_ref, grf_ref, qb_ref, kb_ref, vb_ref, gcb_ref, grb_ref,
                  hf_ref, hb_ref, c_sc, n_sc, m_sc):
    @pl.when(pl.program_id(1) == 0)
    def _():
        c_sc[...] = jnp.zeros(c_sc.shape, F32)
        n_sc[...] = jnp.zeros(n_sc.shape, F32)
        m_sc[...] = jnp.full(m_sc.shape, NEG, F32)

    chains = []
    for h in range(N_HEADS_B):
        qk = slice(h * DK_B, (h + 1) * DK_B)
        vv = slice(h * DV_B, (h + 1) * DV_B)
        chains.append(_mlstm_chunk(qf_ref, kf_ref, vf_ref, qk, vv, gcf_ref[h], grf_ref[h],
                                   c_sc.at[0, h], n_sc.at[0, h], m_sc.at[0, h], hf_ref, reverse=False))
        chains.append(_mlstm_chunk(qb_ref, kb_ref, vb_ref, qk, vv, gcb_ref[h], grb_ref[h],
                                   c_sc.at[1, h], n_sc.at[1, h], m_sc.at[1, h], hb_ref, reverse=True))
    _interleave(chains)


def _mlstm(vqkv, gcol, grow, B, S):
    T = B * S
    nc = S // CHUNK
    q_off = W_A // QK_B
    k_off = (W_A + QK_B) // QK_B
    v_off = (W_A + 2 * QK_B) // W_B

    def fwd(b, c):
        return b * nc + c

    def bwd(b, c):
        return b * nc + nc - 1 - c

    def specs(pos):
        return [
            pl.BlockSpec((CHUNK, QK_B), lambda b, c: (pos(b, c), q_off)),
            pl.BlockSpec((CHUNK, QK_B), lambda b, c: (pos(b, c), k_off)),
            pl.BlockSpec((CHUNK, W_B), lambda b, c: (pos(b, c), v_off)),
            pl.BlockSpec((None, N_HEADS_B, CHUNK, 4), lambda b, c: (b, 0, pos(b, c) - b * nc, 0)),
            pl.BlockSpec((None, N_HEADS_B, 4, CHUNK), lambda b, c: (b, 0, 0, pos(b, c) - b * nc)),
        ]

    out_sds = jax.ShapeDtypeStruct((T, W_B), F32)
    return pl.pallas_call(
        _mlstm_kernel,
        grid=(B, nc),
        in_specs=specs(fwd) + specs(bwd),
        out_specs=[pl.BlockSpec((CHUNK, W_B), lambda b, c: (fwd(b, c), 0)),
                   pl.BlockSpec((CHUNK, W_B), lambda b, c: (bwd(b, c), 0))],
        out_shape=[out_sds, out_sds],
        scratch_shapes=[pltpu.VMEM((2, N_HEADS_B, DK_B, DV_B), F32), pltpu.VMEM((2, N_HEADS_B, 1, DK_B), F32),
                        pltpu.VMEM((2, N_HEADS_B, 1, 1), F32)],
        compiler_params=_params(("parallel", "arbitrary")),
        name="mlstm",
    )(vqkv, vqkv, vqkv, gcol, grow, vqkv, vqkv, vqkv, gcol, grow)


def _mlstm_out_kernel(hf_ref, hb_ref, o_ref, g_ref, y_ref):
    for j in range(N_HEADS_B):
        sl = slice(j * DV_B, (j + 1) * DV_B)
        h = hf_ref[:, sl] + hb_ref[:, sl]
        ms = jnp.mean(h * h, axis=-1, keepdims=True)
        hn = h * lax.rsqrt(ms + EPS) * g_ref[...]
        y_ref[:, sl] = (jax.nn.sigmoid(o_ref[:, sl]) * hn).astype(y_ref.dtype)


def _mlstm_out(hf, hb, om, norm_g, tm):
    T = hf.shape[0]
    row = pl.BlockSpec((tm, W_B), lambda i: (i, 0))
    return pl.pallas_call(
        _mlstm_out_kernel,
        grid=(T // tm,),
        in_specs=[row, row, row, pl.BlockSpec((1, DV_B), lambda i: (0, 0))],
        out_specs=row,
        out_shape=jax.ShapeDtypeStruct((T, W_B), BF16),
        compiler_params=_params(("parallel",)),
        name="mlstm_out",
    )(hf, hb, om, norm_g.reshape(1, DV_B))


def _merge_kernel(ya_ref, yb_ref, wa_ref, wb_ref, ga_ref, gb_ref, o_ref):
    pa = jnp.dot(ya_ref[...], wa_ref[...], preferred_element_type=F32)
    pb = jnp.dot(yb_ref[...], wb_ref[...], preferred_element_type=F32)
    o_ref[...] = (jax.nn.sigmoid(ga_ref[...]) * pa + jax.nn.sigmoid(gb_ref[...]) * pb).astype(o_ref.dtype)


def _merge(ya, yb, wa, wb, gates, tm, tn):
    T = ya.shape[0]
    nb = D_MODEL // tn
    return pl.pallas_call(
        _merge_kernel,
        grid=(T // tm, nb),
        in_specs=[pl.BlockSpec((tm, W_A), lambda i, j: (i, 0)),
                  pl.BlockSpec((tm, W_B), lambda i, j: (i, 0)),
                  pl.BlockSpec((W_A, tn), lambda i, j: (0, j)),
                  pl.BlockSpec((W_B, tn), lambda i, j: (0, j)),
                  pl.BlockSpec((tm, tn), lambda i, j: (i, j)),
                  pl.BlockSpec((tm, tn), lambda i, j: (i, nb + j))],
        out_specs=pl.BlockSpec((tm, tn), lambda i, j: (i, j)),
        out_shape=jax.ShapeDtypeStruct((T, D_MODEL), BF16),
        compiler_params=_params(("parallel", "arbitrary")),
        name="merge",
    )(ya, yb, wa, wb, gates, gates)


def _outproj_kernel(a_ref, w_ref, x_ref, o_ref):
    o_ref[...] = x_ref[...] + jnp.dot(a_ref[...], w_ref[...], preferred_element_type=F32)


def _outproj(merged, w, x, tm, tn):
    T = merged.shape[0]
    return pl.pallas_call(
        _outproj_kernel,
        grid=(T // tm, D_MODEL // tn),
        in_specs=[pl.BlockSpec((tm, D_MODEL), lambda i, j: (i, 0)),
                  pl.BlockSpec((D_MODEL, tn), lambda i, j: (0, j)),
                  pl.BlockSpec((tm, tn), lambda i, j: (i, j))],
        out_specs=pl.BlockSpec((tm, tn), lambda i, j: (i, j)),
        out_shape=jax.ShapeDtypeStruct((T, D_MODEL), F32),
        compiler_params=_params(("parallel", "arbitrary")),
        name="outproj",
    )(merged, w, x)


def _router_kernel(x_ref, g_ref, wr_ref, br_ref, xn_ref, r_ref):
    x = x_ref[...]
    ms = jnp.mean(x * x, axis=-1, keepdims=True)
    xn = x * lax.rsqrt(ms + EPS) * g_ref[...]
    xn_ref[...] = xn
    logits = jnp.dot(xn, wr_ref[...], precision=HIGHEST, preferred_element_type=F32) + br_ref[...]
    lane = lax.broadcasted_iota(jnp.int32, logits.shape, 1)
    cm = jnp.where(lane < N_GROUPS, logits, NEG)
    cmax = jnp.max(cm, axis=-1, keepdims=True)
    p_top = 1.0 / jnp.sum(jnp.exp(cm - cmax), axis=-1, keepdims=True)
    grp = jnp.min(jnp.where(cm == cmax, lane, LANES), axis=-1, keepdims=True)
    lo = N_GROUPS + EXP_PER_GROUP * grp
    fm = jnp.where((lane >= lo) & (lane < lo + EXP_PER_GROUP), logits, NEG)
    fmax = jnp.max(fm, axis=-1, keepdims=True)
    fe = jnp.exp(fm - fmax)
    pe = fe / jnp.sum(fe, axis=-1, keepdims=True)
    i1 = jnp.min(jnp.where(fm == fmax, lane, LANES), axis=-1, keepdims=True)
    p1 = jnp.sum(jnp.where(lane == i1, pe, 0.0), axis=-1, keepdims=True)
    fm2 = jnp.where(lane == i1, NEG, fm)
    f2max = jnp.max(fm2, axis=-1, keepdims=True)
    i2 = jnp.min(jnp.where(fm2 == f2max, lane, LANES), axis=-1, keepdims=True)
    p2 = jnp.sum(jnp.where(lane == i2, pe, 0.0), axis=-1, keepdims=True)
    g1 = p_top * (p1 / (p1 + p2))
    g2 = p_top * (p2 / (p1 + p2))
    e1 = (i1 - N_GROUPS).astype(F32)
    e2 = (i2 - N_GROUPS).astype(F32)
    r_ref[...] = jnp.where(lane == 0, e1, jnp.where(lane == 1, e2, jnp.where(lane == 2, g1,
                           jnp.where(lane == 3, g2, 0.0))))


def _router(x1, g, wr, br, tm):
    T = x1.shape[0]
    return pl.pallas_call(
        _router_kernel,
        grid=(T // tm,),
        in_specs=[pl.BlockSpec((tm, D_MODEL), lambda i: (i, 0)),
                  pl.BlockSpec((1, D_MODEL), lambda i: (0, 0)),
                  pl.BlockSpec((D_MODEL, LANES), lambda i: (0, 0)),
                  pl.BlockSpec((1, LANES), lambda i: (0, 0))],
        out_specs=[pl.BlockSpec((tm, D_MODEL), lambda i: (i, 0)),
                   pl.BlockSpec((tm, LANES), lambda i: (i, 0))],
        out_shape=[jax.ShapeDtypeStruct((T, D_MODEL), F32), jax.ShapeDtypeStruct((T, LANES), F32)],
        compiler_params=_params(("parallel",)),
        name="router",
    )(x1, g.reshape(1, D_MODEL), wr, br)


def _row_copy(src_hbm, row, dst, r, sem):
    return pltpu.make_async_copy(src_hbm.at[pl.ds(row, 1)], dst.at[pl.ds(r, 1)], sem)


def _moe_kernel(tok_ref, be_ref, nused_ref, x_hbm, w1_ref, w3_ref, w2_ref, o_ref, xbuf, sems):
    b = pl.program_id(0)
    nused = nused_ref[0]
    cur = lax.rem(b, 2)

    def gather(blk, buf, fn):
        for r in range(MOE_ROWS):
            fn(_row_copy(x_hbm, tok_ref[blk * MOE_ROWS + r], xbuf.at[buf], r, sems.at[buf]))

    def mlp(prefetch_next):
        gather(b, cur, lambda cp: cp.wait())
        if prefetch_next:
            gather(b + 1, 1 - cur, lambda cp: cp.start())
        x = xbuf[cur].astype(BF16)
        h1 = jnp.dot(x, w1_ref[...], preferred_element_type=F32)
        h3 = jnp.dot(x, w3_ref[...], preferred_element_type=F32)
        hid = (h1 * jax.nn.sigmoid(h1)) * h3
        o_ref[...] = jnp.dot(hid.astype(BF16), w2_ref[...], preferred_element_type=F32)

    @pl.when(b == 0)
    def _():
        gather(0, 0, lambda cp: cp.start())

    @pl.when(b + 1 < nused)
    def _():
        mlp(True)

    @pl.when(b + 1 == nused)
    def _():
        mlp(False)

    @pl.when(b >= nused)
    def _():
        o_ref[...] = jnp.zeros(o_ref.shape, o_ref.dtype)


def _moe_mlp(xn, slot_tok, w1, w3, w2, blk_exp, nused):
    ns = slot_tok.shape[0]
    nb = ns // MOE_ROWS
    once = pl.Buffered(1)
    return pl.pallas_call(
        _moe_kernel,
        grid_spec=pltpu.PrefetchScalarGridSpec(
            num_scalar_prefetch=3,
            grid=(nb,),
            in_specs=[
                pl.BlockSpec(memory_space=pl.ANY),
                pl.BlockSpec((None, D_MODEL, D_FF_EXP), lambda b, tok, be, nu: (be[b], 0, 0), pipeline_mode=once),
                pl.BlockSpec((None, D_MODEL, D_FF_EXP), lambda b, tok, be, nu: (be[b], 0, 0), pipeline_mode=once),
                pl.BlockSpec((None, D_FF_EXP, D_MODEL), lambda b, tok, be, nu: (be[b], 0, 0), pipeline_mode=once),
            ],
            out_specs=pl.BlockSpec((MOE_ROWS, D_MODEL), lambda b, tok, be, nu: (b, 0)),
            scratch_shapes=[pltpu.VMEM((2, MOE_ROWS, D_MODEL), F32), pltpu.SemaphoreType.DMA((2,))],
        ),
        out_shape=jax.ShapeDtypeStruct((ns, D_MODEL), F32),
        compiler_params=_params(("arbitrary",)),
        name="moe_mlp",
    )(slot_tok, blk_exp, nused, xn, w1, w3, w2)


def _combine_kernel(pos_ref, x_ref, r_ref, o_hbm, y_ref, buf, sems, *, rows):
    i = pl.program_id(0)
    cur = lax.rem(i, 2)

    def gather(step, slot, fn):
        for r in range(rows):
            for k in range(2):
                fn(_row_copy(o_hbm, pos_ref[2 * (step * rows + r) + k], buf.at[slot, k], r, sems.at[slot]))

    def combine(prefetch_next):
        gather(i, cur, lambda cp: cp.wait())
        if prefetch_next:
            gather(i + 1, 1 - cur, lambda cp: cp.start())
        r = r_ref[...]
        y_ref[...] = x_ref[...] + (buf[cur, 0] * r[:, 2:3] + buf[cur, 1] * r[:, 3:4])

    @pl.when(i == 0)
    def _():
        gather(0, 0, lambda cp: cp.start())

    @pl.when(i + 1 < pl.num_programs(0))
    def _():
        combine(True)

    @pl.when(i + 1 == pl.num_programs(0))
    def _():
        combine(False)


def _combine(x1, route, o, pos, rows):
    T, D = x1.shape
    return pl.pallas_call(
        functools.partial(_combine_kernel, rows=rows),
        grid_spec=pltpu.PrefetchScalarGridSpec(
            num_scalar_prefetch=1,
            grid=(T // rows,),
            in_specs=[pl.BlockSpec((rows, D), lambda i, pos: (i, 0)),
                      pl.BlockSpec((rows, LANES), lambda i, pos: (i, 0)),
                      pl.BlockSpec(memory_space=pl.ANY)],
            out_specs=pl.BlockSpec((rows, D), lambda i, pos: (i, 0)),
            scratch_shapes=[pltpu.VMEM((2, 2, rows, D), F32), pltpu.SemaphoreType.DMA((2,))],
        ),
        out_shape=jax.ShapeDtypeStruct((T, D), F32),
        compiler_params=_params(("arbitrary",)),
        name="moe_combine",
    )(pos, x1, route, o)


def _dispatch_plan(route, T):
    i32 = jnp.int32
    tk = 2 * T
    nb = tk // MOE_ROWS + N_EXP
    experts = jnp.arange(N_EXP, dtype=i32)
    e_flat = route[:, 0:2].astype(i32).reshape(-1)
    order = jnp.argsort(e_flat).astype(i32)
    rank = jnp.argsort(order).astype(i32)
    counts = jnp.sum(e_flat[:, None] == experts[None, :], axis=0, dtype=i32)
    pcounts = (counts + MOE_ROWS - 1) // MOE_ROWS * MOE_ROWS
    pends = jnp.cumsum(pcounts)
    pstarts = pends - pcounts
    ustarts = jnp.cumsum(counts) - counts
    pos = pstarts[e_flat] + rank - ustarts[e_flat]
    slot = jnp.arange(nb * MOE_ROWS, dtype=i32)
    slot_exp = jnp.minimum(jnp.sum(pends[None, :] <= slot[:, None], axis=1, dtype=i32), N_EXP - 1)
    within = slot - pstarts[slot_exp]
    src = jnp.clip(ustarts[slot_exp] + within, 0, tk - 1)
    slot_tok = jnp.where(within < counts[slot_exp], order[src] // 2, 0).astype(i32)
    blk_exp = slot_exp[::MOE_ROWS]
    nused = (pends[-1:] // MOE_ROWS).astype(i32)
    return slot_tok, blk_exp, nused, pos.astype(i32)


def _tile(n, pref):
    return pref if n % pref == 0 else n


def _trunk(x, p):
    B, S, D = x.shape
    T = B * S
    xt = x.reshape(T, D)
    tm = _tile(T, PROJ_TM)
    tn = PROJ_TN
    tr = _tile(T, ROW_TM)

    xn = _rmsnorm(xt, p["norm_mix_g"], tr)
    q = _proj(xn, p["w_main"], W_A, BF16, tm, tn, col_off=0, vec=p["q_gain"], kind="headnorm", name="proj_q")
    kt = _proj(xn, p["w_main"], W_A, BF16, tm, tn, col_off=W_A, vec=p["k_gain"], kind="headnorm_t",
               name="proj_kt")
    vqkv = _proj(xn, p["w_main"], W_A + 2 * QK_B + W_B, BF16, tm, tn, col_off=2 * W_A, name="proj_vqkv")
    om = _proj(xn, p["w_main"], W_B, F32, tm, tn, col_off=3 * W_A + 2 * QK_B + W_B, name="proj_o")
    gates = _proj(xn, p["w_gate"], 2 * D_MODEL, F32, tm, tn, name="proj_gates")
    gif = _proj(xn, p["w_gif"], LANES, F32, tm, LANES, vec=p["b_gif"], kind="bias", name="proj_gif")

    ya = _diff_attention(q, kt, vqkv, p["slopes"], p["lam"], p["subln_scaled"], B, S,
                         _tile(S, ATTN_TQ), _tile(S, ATTN_TK))

    g4 = gif[:, :4 * N_HEADS_B].reshape(B, S, 4, N_HEADS_B)
    gcol = g4.transpose(0, 3, 1, 2)
    grow = g4.transpose(0, 3, 2, 1)
    hf, hb = _mlstm(vqkv, gcol, grow, B, S)
    yb = _mlstm_out(hf, hb, om, p["mlstm_norm_g"], tr)

    merged = _merge(ya, yb, p["w_branch_a"], p["w_branch_b"], gates, tm, MERGE_TN)
    x1 = _outproj(merged, p["w_out"], xt, tm, MERGE_TN)

    xn2, route = _router(x1, p["norm_ffn_g"], p["w_router"], p["b_router"], tr)
    slot_tok, blk_exp, nused, pos = _dispatch_plan(route, T)
    o = _moe_mlp(xn2, slot_tok, p["w1"], p["w3"], p["w2"], blk_exp, nused)
    y = _combine(x1, route, o, pos, _tile(T, 128))
    return y.reshape(B, S, D)


def _prepare(norm_mix_g, w_in, b_mgate, attn_q_norm_g, attn_k_norm_g, lambda_q1, lambda_k1, lambda_q2,
             lambda_k2, attn_subln_g, mlstm_norm_g, w_branch_a, w_branch_b, w_out, norm_ffn_g,
             w_router_group, b_router_group, w_router_expert, b_router_expert, w_exp_gate, w_exp_up,
             w_exp_down):
    l = 0
    n_main = 3 * W_A + 2 * QK_B + 2 * W_B
    n_if = 4 * N_HEADS_B
    w = w_in[l]
    q_gain = jnp.tile((attn_q_norm_g[l].astype(F32) * (HEAD_DIM_A ** -0.5 * LOG2E)).reshape(-1), N_HEADS_A)
    k_gain = jnp.tile(attn_k_norm_g[l].astype(F32).reshape(-1), N_HEADS_A)
    lam = (jnp.exp(jnp.sum(lambda_q1[l].astype(F32) * lambda_k1[l].astype(F32)))
           - jnp.exp(jnp.sum(lambda_q2[l].astype(F32) * lambda_k2[l].astype(F32))) + LAM_INIT)
    slopes = LOG2E * 2.0 ** (-8.0 * (jnp.arange(N_HEADS_A, dtype=F32) + 1.0) / N_HEADS_A)
    w_router = jnp.zeros((D_MODEL, LANES), F32)
    w_router = w_router.at[:, :N_GROUPS].set(w_router_group[l])
    w_router = w_router.at[:, N_GROUPS:N_GROUPS + N_EXP].set(w_router_expert[l])
    b_router = jnp.zeros((1, LANES), F32)
    b_router = b_router.at[0, :N_GROUPS].set(b_router_group[l].astype(F32))
    b_router = b_router.at[0, N_GROUPS:N_GROUPS + N_EXP].set(b_router_expert[l].astype(F32).reshape(-1))
    return {
        "norm_mix_g": norm_mix_g[l],
        "w_main": w[:, :n_main].astype(BF16),
        "w_gate": w[:, n_main + n_if:].astype(BF16),
        "w_gif": jnp.pad(w[:, n_main:n_main + n_if], ((0, 0), (0, LANES - n_if))).astype(BF16),
        "b_gif": jnp.pad(b_mgate[l].astype(F32), (0, LANES - n_if)),
        "q_gain": q_gain,
        "k_gain": k_gain,
        "lam": lam.reshape(1),
        "slopes": slopes,
        "subln_scaled": attn_subln_g[l].astype(F32) * (1.0 - LAM_INIT),
        "mlstm_norm_g": mlstm_norm_g[l],
        "w_branch_a": w_branch_a[l].astype(BF16),
        "w_branch_b": w_branch_b[l].astype(BF16),
        "w_out": w_out[l].astype(BF16),
        "norm_ffn_g": norm_ffn_g[l],
        "w_router": w_router,
        "b_router": b_router,
        "w1": w_exp_gate[l].astype(BF16),
        "w3": w_exp_up[l].astype(BF16),
        "w2": w_exp_down[l].astype(BF16),
    }


def kernel(x_prompt, x_sample, norm_mix_g, w_in, b_mgate, attn_q_norm_g, attn_k_norm_g, lambda_q1, lambda_k1, lambda_q2, lambda_k2, attn_subln_g, mlstm_norm_g, w_branch_a, w_branch_b, w_out, norm_ffn_g, w_router_group, b_router_group, w_router_expert, b_router_expert, w_exp_gate, w_exp_up, w_exp_down):
    p = _prepare(norm_mix_g, w_in, b_mgate, attn_q_norm_g, attn_k_norm_g, lambda_q1, lambda_k1, lambda_q2,
                 lambda_k2, attn_subln_g, mlstm_norm_g, w_branch_a, w_branch_b, w_out, norm_ffn_g,
                 w_router_group, b_router_group, w_router_expert, b_router_expert, w_exp_gate, w_exp_up,
                 w_exp_down)
    return (_trunk(x_prompt, p), _trunk(x_sample, p))
```

```python
import functools

import jax
import jax.numpy as jnp
from jax import lax
from jax.experimental import pallas as pl
from jax.experimental.pallas import tpu as pltpu

D_MODEL = 4096
HEAD_DIM_A = 128
N_HEADS_A = 8
W_A = 2048
N_HEADS_B = 4
DV_B = 512
DK_B = 256
W_B = 2048
QK_B = 1024
CHUNK = 128
N_GROUPS = 4
EXP_PER_GROUP = 8
N_EXP = 32
D_FF_EXP = 1024
EPS = 1e-6
NEG = -1e30
LAM_INIT = 0.8 - 0.6 * 1.0

LANES = 128
MOE_ROWS = 256
PROJ_TM, PROJ_TN = 1024, 1024
MERGE_TN = 512
ROW_TM = 256
ATTN_TQ, ATTN_TK = 512, 2048
ATTN_UNIT = 256
VMEM_LIMIT = 56 * 1024 * 1024
ATTN_ROWS = 16
LOG2E = 1.4426950408889634

F32 = jnp.float32
BF16 = jnp.bfloat16
HIGHEST = lax.Precision.HIGHEST


def _params(sem, vmem=VMEM_LIMIT):
    return pltpu.CompilerParams(dimension_semantics=sem, vmem_limit_bytes=vmem)


def _rmsnorm_kernel(x_ref, g_ref, o_ref):
    x = x_ref[...]
    ms = jnp.mean(x * x, axis=-1, keepdims=True)
    o_ref[...] = (x * lax.rsqrt(ms + EPS) * g_ref[...]).astype(o_ref.dtype)


def _rmsnorm(x, g, tm):
    T, D = x.shape
    return pl.pallas_call(
        _rmsnorm_kernel,
        grid=(T // tm,),
        in_specs=[pl.BlockSpec((tm, D), lambda i: (i, 0)), pl.BlockSpec((1, D), lambda i: (0, 0))],
        out_specs=pl.BlockSpec((tm, D), lambda i: (i, 0)),
        out_shape=jax.ShapeDtypeStruct((T, D), BF16),
        compiler_params=_params(("parallel",)),
        name="rmsnorm",
    )(x, g.reshape(1, D))


def _proj_kernel(a_ref, w_ref, o_ref):
    o_ref[...] = jnp.dot(a_ref[...], w_ref[...], preferred_element_type=F32).astype(o_ref.dtype)


def _proj_bias_kernel(a_ref, w_ref, b_ref, o_ref):
    acc = jnp.dot(a_ref[...], w_ref[...], preferred_element_type=F32)
    o_ref[...] = (acc + b_ref[...]).astype(o_ref.dtype)


def _proj_headnorm_kernel(a_ref, w_ref, g_ref, o_ref, *, transposed):
    acc = jnp.dot(a_ref[...], w_ref[...], preferred_element_type=F32)
    for j in range(acc.shape[1] // HEAD_DIM_A):
        sl = slice(j * HEAD_DIM_A, (j + 1) * HEAD_DIM_A)
        blk = acc[:, sl]
        ms = jnp.mean(blk * blk, axis=-1, keepdims=True)
        y = blk * lax.rsqrt(ms + EPS) * g_ref[:, sl]
        if transposed:
            o_ref[sl, :] = y.T.astype(o_ref.dtype)
        else:
            o_ref[:, sl] = y.astype(o_ref.dtype)


def _proj(a, w, n_out, out_dtype, tm, tn, *, col_off=0, vec=None, kind="plain", name="proj"):
    T, K = a.shape
    assert n_out % tn == 0 and col_off % tn == 0 and T % tm == 0
    off = col_off // tn
    in_specs = [pl.BlockSpec((tm, K), lambda i, j: (i, 0)),
                pl.BlockSpec((K, tn), lambda i, j: (0, j + off))]
    args = [a, w]
    if kind == "plain":
        body = _proj_kernel
    else:
        body = (_proj_bias_kernel if kind == "bias" else
                functools.partial(_proj_headnorm_kernel, transposed=(kind == "headnorm_t")))
        in_specs.append(pl.BlockSpec((1, tn), lambda i, j: (0, j)))
        args.append(vec.reshape(1, n_out).astype(F32))
    if kind == "headnorm_t":
        out_spec = pl.BlockSpec((tn, tm), lambda i, j: (j, i))
        out_shape = jax.ShapeDtypeStruct((n_out, T), out_dtype)
    else:
        out_spec = pl.BlockSpec((tm, tn), lambda i, j: (i, j))
        out_shape = jax.ShapeDtypeStruct((T, n_out), out_dtype)
    return pl.pallas_call(
        body,
        grid=(T // tm, n_out // tn),
        in_specs=in_specs,
        out_specs=out_spec,
        out_shape=out_shape,
        compiler_params=_params(("parallel", "arbitrary")),
        name=name,
    )(*args)


def _tree_reduce(op, xs):
    while len(xs) > 1:
        xs = [op(xs[k], xs[k + 1]) if k + 1 < len(xs) else xs[k] for k in range(0, len(xs), 2)]
    return xs[0]


def _attn_kernel(slopes_ref, lam_ref, q_ref, kt_ref, v_ref, g_ref, o_ref,
                 s_sc, p_sc, pm_sc, ps_sc, e_sc, m_sc, l_sc, acc_sc, *, tq, tk):
    h = pl.program_id(1)
    qi = pl.program_id(2)
    ki = pl.program_id(3)
    rows_per = ATTN_ROWS
    n_lane_tiles = tk // LANES

    @pl.when(ki == 0)
    def _():
        m_sc[...] = jnp.full(m_sc.shape, NEG, F32)
        l_sc[...] = jnp.zeros(l_sc.shape, F32)
        acc_sc[...] = jnp.zeros(acc_sc.shape, F32)

    slope = slopes_ref[h]
    off = qi * tq - ki * tk
    left = off >= tk - 1
    right = off <= 1 - tq
    separable = left | right
    col = lax.broadcasted_iota(jnp.int32, (1, tk), 1)
    row = lax.broadcasted_iota(jnp.int32, (rows_per, 1), 0)
    all_rows = lax.broadcasted_iota(jnp.int32, (tq, 1), 0)
    col_bias = slope * jnp.where(left, tk - 1 - col, col).astype(F32)
    row_term = jnp.where(left, all_rows + (off - (tk - 1)), -off - all_rows).astype(F32)
    shift = jnp.where(separable, -slope * row_term, 0.0)

    def lane_tile(j):
        return slice(j * LANES, (j + 1) * LANES)

    def row_group(i):
        return slice(i * rows_per, (i + 1) * rows_per)

    def pass_a(c, first, is_separable):
        for i in range(first, first + ATTN_UNIT // rows_per):
            rows = row_group(i)
            tiles = []
            for j in range(n_lane_tiles):
                if is_separable:
                    t = s_sc[c, rows, lane_tile(j)] - col_bias[:, lane_tile(j)]
                else:
                    d = (row + (i * rows_per + off)) - col[:, lane_tile(j)]
                    t = s_sc[c, rows, lane_tile(j)] - slope * jnp.abs(d.astype(F32))
                s_sc[c, rows, lane_tile(j)] = t
                tiles.append(t)
            pm_sc[c, rows, :] = _tree_reduce(jnp.maximum, tiles)

    def pass_b(c, first):
        for i in range(first, first + ATTN_UNIT // rows_per):
            rows = row_group(i)
            e = e_sc[c, rows, :]
            tiles = []
            for j in range(n_lane_tiles):
                p = jnp.exp2(s_sc[c, rows, lane_tile(j)] - e)
                p_sc[c, rows, lane_tile(j)] = p.astype(BF16)
                tiles.append(p)
            ps_sc[c, rows, :] = _tree_reduce(jnp.add, tiles)

    def tile(is_separable):
        v = v_ref[...]
        units = [(c, u) for u in range(tq // ATTN_UNIT) for c in range(2)]
        for c, u in units:
            rows = slice(u * ATTN_UNIT, (u + 1) * ATTN_UNIT)
            sl = slice(c * HEAD_DIM_A, (c + 1) * HEAD_DIM_A)
            s_sc[c, rows, :] = jnp.dot(q_ref[rows, sl], kt_ref[sl, :], preferred_element_type=F32)
        for c, u in units:
            rows = slice(u * ATTN_UNIT, (u + 1) * ATTN_UNIT)
            first = u * (ATTN_UNIT // rows_per)
            pass_a(c, first, is_separable)
            m_prev = m_sc[c, rows, :]
            m_new = jnp.maximum(m_prev, jnp.max(pm_sc[c, rows, :], axis=-1, keepdims=True) + shift[rows, :])
            alpha = jnp.exp2(m_prev - m_new)
            m_sc[c, rows, :] = m_new
            e_sc[c, rows, :] = jnp.broadcast_to(m_new - shift[rows, :], (ATTN_UNIT, LANES))
            pass_b(c, first)
            l_sc[c, rows, :] = alpha * l_sc[c, rows, :] + jnp.sum(ps_sc[c, rows, :], axis=-1, keepdims=True)
            acc_sc[c, rows, :] = (alpha * acc_sc[c, rows, :]
                                  + jnp.dot(p_sc[c, rows, :], v, preferred_element_type=F32))

    @pl.when(separable)
    def _():
        tile(True)

    @pl.when(jnp.logical_not(separable))
    def _():
        tile(False)

    @pl.when(ki == pl.num_programs(3) - 1)
    def _():
        o = acc_sc[0] / l_sc[0] - lam_ref[0] * (acc_sc[1] / l_sc[1])
        ms = jnp.mean(o * o, axis=-1, keepdims=True)
        o_ref[...] = (o * lax.rsqrt(ms + EPS) * g_ref[...]).astype(o_ref.dtype)


def _diff_attention(q, kt, vqkv, slopes, lam, subln_scaled, B, S, tq, tk):
    T = B * S
    nq, nk = S // tq, S // tk
    hw = 2 * HEAD_DIM_A
    assert tq % ATTN_UNIT == 0 and ATTN_UNIT % ATTN_ROWS == 0 and tk % LANES == 0
    return pl.pallas_call(
        functools.partial(_attn_kernel, tq=tq, tk=tk),
        grid=(B, N_HEADS_A, nq, nk),
        in_specs=[
            pl.BlockSpec(memory_space=pltpu.SMEM),
            pl.BlockSpec(memory_space=pltpu.SMEM),
            pl.BlockSpec((tq, hw), lambda b, h, i, j: (b * nq + i, h)),
            pl.BlockSpec((hw, tk), lambda b, h, i, j: (h, b * nk + j)),
            pl.BlockSpec((tk, hw), lambda b, h, i, j: (b * nk + j, h)),
            pl.BlockSpec((1, hw), lambda b, h, i, j: (0, 0)),
        ],
        out_specs=pl.BlockSpec((tq, hw), lambda b, h, i, j: (b * nq + i, h)),
        out_shape=jax.ShapeDtypeStruct((T, W_A), BF16),
        scratch_shapes=[pltpu.VMEM((2, tq, tk), F32), pltpu.VMEM((2, tq, tk), BF16),
                        pltpu.VMEM((2, tq, LANES), F32), pltpu.VMEM((2, tq, LANES), F32),
                        pltpu.VMEM((2, tq, LANES), F32),
                        pltpu.VMEM((2, tq, 1), F32), pltpu.VMEM((2, tq, 1), F32),
                        pltpu.VMEM((2, tq, hw), F32)],
        compiler_params=_params(("parallel", "parallel", "parallel", "arbitrary")),
        name="diff_attention",
    )(slopes, lam, q, kt, vqkv, subln_scaled.reshape(1, hw))


def _log_sigmoid(x):
    return jnp.minimum(x, 0.0) - jnp.log1p(jnp.exp(-jnp.abs(x)))


def _mlstm_chunk(q_ref, k_ref, v_ref, qk_cols, v_cols, gcol, grow, c_ref, n_ref, m_ref, h_ref, reverse):
    L = q_ref.shape[0]
    ii, fi = (2, 3) if reverse else (0, 1)
    r = lax.broadcasted_iota(jnp.int32, (L, L), 0)
    c = lax.broadcasted_iota(jnp.int32, (L, L), 1)
    lower = c <= r
    upper = c >= r
    tri_lo = jnp.where(lower, 1.0, 0.0).astype(F32)
    tri_up = jnp.where(upper, 1.0, 0.0).astype(F32)
    ic_row = grow[ii:ii + 1, :]
    ic_col = gcol[:, ii:ii + 1]
    lf_row = _log_sigmoid(grow[fi:fi + 1, :])
    lf_col = _log_sigmoid(gcol[:, fi:fi + 1])
    if reverse:
        b_row = jnp.dot(lf_row, tri_lo, precision=HIGHEST, preferred_element_type=F32)
        b_col = jnp.dot(tri_up, lf_col, precision=HIGHEST, preferred_element_type=F32)
        mask = upper
        b_last = b_row[:, 0:1]
    else:
        b_row = jnp.dot(lf_row, tri_up, precision=HIGHEST, preferred_element_type=F32)
        b_col = jnp.dot(tri_lo, lf_col, precision=HIGHEST, preferred_element_type=F32)
        mask = lower
        b_last = b_row[:, L - 1:L]
    qs = q_ref[:, qk_cols] * (DK_B ** -0.5)
    k = k_ref[:, qk_cols]
    s_raw = lax.dot_general(qs, k, (((1,), (1,)), ((), ())), preferred_element_type=F32)
    q_c = jnp.dot(qs, c_ref[...].astype(BF16), preferred_element_type=F32)
    yield
    m = m_ref[...]
    log_d = jnp.where(mask, b_col - b_row + ic_row, NEG)
    inter = b_col + m
    m_row = jnp.maximum(inter, jnp.max(log_d, axis=-1, keepdims=True))
    d_m = jnp.exp(log_d - m_row)
    sc = jnp.exp(inter - m_row)
    w_col = b_last - b_col + ic_col
    w_row = b_last - b_row + ic_row
    m_new = jnp.maximum(b_last + m, jnp.max(w_row, axis=-1, keepdims=True))
    decay = jnp.exp(b_last + m - m_new)
    ws_col = jnp.exp(w_col - m_new)
    s = s_raw * d_m
    v = v_ref[:, v_cols]
    wv = (ws_col * v.astype(F32)).astype(BF16)
    yield
    s_v = jnp.dot(s.astype(BF16), v, preferred_element_type=F32)
    k_wv = lax.dot_general(k, wv, (((0,), (0,)), ((), ())), preferred_element_type=F32)
    yield
    n_state = n_ref[...]
    num = s_v + sc * q_c
    den = (jnp.sum(s, axis=-1, keepdims=True)
           + sc * jnp.sum(qs.astype(F32) * n_state, axis=-1, keepdims=True))
    h_ref[:, v_cols] = num / jnp.maximum(jnp.abs(den), jnp.exp(-m_row))
    c_ref[...] = decay * c_ref[...] + k_wv
    n_ref[...] = decay * n_state + jnp.sum(ws_col * k.astype(F32), axis=0, keepdims=True)
    m_ref[...] = m_new


def _interleave(chains):
    chains = list(chains)
    while chains:
        alive = []
        for g in chains:
            try:
                next(g)
                alive.append(g)
            except StopIteration:
                pass
        chains = alive


def _mlstm_kernel(qf_ref, kf_ref, vf_ref, gcf_ref, grf_ref, qb_ref, kb_ref, vb_ref, gcb_ref, grb_ref,
                  hf_ref, hb_ref, c_sc, n_sc, m_sc):
    @pl.when(pl.program_id(1) == 0)
    def _():
        c_sc[...] = jnp.zeros(c_sc.shape, F32)
        n_sc[...] = jnp.zeros(n_sc.shape, F32)
        m_sc[...] = jnp.full(m_sc.shape, NEG, F32)

    chains = []
    for h in range(N_HEADS_B):
        qk = slice(h * DK_B, (h + 1) * DK_B)
        vv = slice(h * DV_B, (h + 1) * DV_B)
        chains.append(_mlstm_chunk(qf_ref, kf_ref, vf_ref, qk, vv, gcf_ref[h], grf_ref[h],
                                   c_sc.at[0, h], n_sc.at[0, h], m_sc.at[0, h], hf_ref, reverse=False))
        chains.append(_mlstm_chunk(qb_ref, kb_ref, vb_ref, qk, vv, gcb_ref[h], grb_ref[h],
                                   c_sc.at[1, h], n_sc.at[1, h], m_sc.at[1, h], hb_ref, reverse=True))
    _interleave(chains)


def _mlstm(vqkv, gcol, grow, B, S):
    T = B * S
    nc = S // CHUNK
    q_off = W_A // QK_B
    k_off = (W_A + QK_B) // QK_B
    v_off = (W_A + 2 * QK_B) // W_B

    def fwd(b, c):
        return b * nc + c

    def bwd(b, c):
        return b * nc + nc - 1 - c

    def specs(pos):
        return [
            pl.BlockSpec((CHUNK, QK_B), lambda b, c: (pos(b, c), q_off)),
            pl.BlockSpec((CHUNK, QK_B), lambda b, c: (pos(b, c), k_off)),
            pl.BlockSpec((CHUNK, W_B), lambda b, c: (pos(b, c), v_off)),
            pl.BlockSpec((None, N_HEADS_B, CHUNK, 4), lambda b, c: (b, 0, pos(b, c) - b * nc, 0)),
            pl.BlockSpec((None, N_HEADS_B, 4, CHUNK), lambda b, c: (b, 0, 0, pos(b, c) - b * nc)),
        ]

    out_sds = jax.ShapeDtypeStruct((T, W_B), F32)
    return pl.pallas_call(
        _mlstm_kernel,
        grid=(B, nc),
        in_specs=specs(fwd) + specs(bwd),
        out_specs=[pl.BlockSpec((CHUNK, W_B), lambda b, c: (fwd(b, c), 0)),
                   pl.BlockSpec((CHUNK, W_B), lambda b, c: (bwd(b, c), 0))],
        out_shape=[out_sds, out_sds],
        scratch_shapes=[pltpu.VMEM((2, N_HEADS_B, DK_B, DV_B), F32), pltpu.VMEM((2, N_HEADS_B, 1, DK_B), F32),
                        pltpu.VMEM((2, N_HEADS_B, 1, 1), F32)],
        compiler_params=_params(("parallel", "arbitrary")),
        name="mlstm",
    )(vqkv, vqkv, vqkv, gcol, grow, vqkv, vqkv, vqkv, gcol, grow)


def _mlstm_out_kernel(hf_ref, hb_ref, o_ref, g_ref, y_ref):
    for j in range(N_HEADS_B):
        sl = slice(j * DV_B, (j + 1) * DV_B)
        h = hf_ref[:, sl] + hb_ref[:, sl]
        ms = jnp.mean(h * h, axis=-1, keepdims=True)
        hn = h * lax.rsqrt(ms + EPS) * g_ref[...]
        y_ref[:, sl] = (jax.nn.sigmoid(o_ref[:, sl]) * hn).astype(y_ref.dtype)


def _mlstm_out(hf, hb, om, norm_g, tm):
    T = hf.shape[0]
    row = pl.BlockSpec((tm, W_B), lambda i: (i, 0))
    return pl.pallas_call(
        _mlstm_out_kernel,
        grid=(T // tm,),
        in_specs=[row, row, row, pl.BlockSpec((1, DV_B), lambda i: (0, 0))],
        out_specs=row,
        out_shape=jax.ShapeDtypeStruct((T, W_B), BF16),
        compiler_params=_params(("parallel",)),
        name="mlstm_out",
    )(hf, hb, om, norm_g.reshape(1, DV_B))


def _merge_kernel(ya_ref, yb_ref, wa_ref, wb_ref, ga_ref, gb_ref, o_ref):
    pa = jnp.dot(ya_ref[...], wa_ref[...], preferred_element_type=F32)
    pb = jnp.dot(yb_ref[...], wb_ref[...], preferred_element_type=F32)
    o_ref[...] = (jax.nn.sigmoid(ga_ref[...]) * pa + jax.nn.sigmoid(gb_ref[...]) * pb).astype(o_ref.dtype)


def _merge(ya, yb, wa, wb, gates, tm, tn):
    T = ya.shape[0]
    nb = D_MODEL // tn
    return pl.pallas_call(
        _merge_kernel,
        grid=(T // tm, nb),
        in_specs=[pl.BlockSpec((tm, W_A), lambda i, j: (i, 0)),
                  pl.BlockSpec((tm, W_B), lambda i, j: (i, 0)),
                  pl.BlockSpec((W_A, tn), lambda i, j: (0, j)),
                  pl.BlockSpec((W_B, tn), lambda i, j: (0, j)),
                  pl.BlockSpec((tm, tn), lambda i, j: (i, j)),
                  pl.BlockSpec((tm, tn), lambda i, j: (i, nb + j))],
        out_specs=pl.BlockSpec((tm, tn), lambda i, j: (i, j)),
        out_shape=jax.ShapeDtypeStruct((T, D_MODEL), BF16),
        compiler_params=_params(("parallel", "arbitrary")),
        name="merge",
    )(ya, yb, wa, wb, gates, gates)


def _outproj_kernel(a_ref, w_ref, x_ref, o_ref):
    o_ref[...] = x_ref[...] + jnp.dot(a_ref[...], w_ref[...], preferred_element_type=F32)


def _outproj(merged, w, x, tm, tn):
    T = merged.shape[0]
    return pl.pallas_call(
        _outproj_kernel,
        grid=(T // tm, D_MODEL // tn),
        in_specs=[pl.BlockSpec((tm, D_MODEL), lambda i, j: (i, 0)),
                  pl.BlockSpec((D_MODEL, tn), lambda i, j: (0, j)),
                  pl.BlockSpec((tm, tn), lambda i, j: (i, j))],
        out_specs=pl.BlockSpec((tm, tn), lambda i, j: (i, j)),
        out_shape=jax.ShapeDtypeStruct((T, D_MODEL), F32),
        compiler_params=_params(("parallel", "arbitrary")),
        name="outproj",
    )(merged, w, x)


def _router_kernel(x_ref, g_ref, wr_ref, br_ref, xn_ref, r_ref):
    x = x_ref[...]
    ms = jnp.mean(x * x, axis=-1, keepdims=True)
    xn = x * lax.rsqrt(ms + EPS) * g_ref[...]
    xn_ref[...] = xn
    logits = jnp.dot(xn, wr_ref[...], precision=HIGHEST, preferred_element_type=F32) + br_ref[...]
    lane = lax.broadcasted_iota(jnp.int32, logits.shape, 1)
    cm = jnp.where(lane < N_GROUPS, logits, NEG)
    cmax = jnp.max(cm, axis=-1, keepdims=True)
    p_top = 1.0 / jnp.sum(jnp.exp(cm - cmax), axis=-1, keepdims=True)
    grp = jnp.min(jnp.where(cm == cmax, lane, LANES), axis=-1, keepdims=True)
    lo = N_GROUPS + EXP_PER_GROUP * grp
    fm = jnp.where((lane >= lo) & (lane < lo + EXP_PER_GROUP), logits, NEG)
    fmax = jnp.max(fm, axis=-1, keepdims=True)
    fe = jnp.exp(fm - fmax)
    pe = fe / jnp.sum(fe, axis=-1, keepdims=True)
    i1 = jnp.min(jnp.where(fm == fmax, lane, LANES), axis=-1, keepdims=True)
    p1 = jnp.sum(jnp.where(lane == i1, pe, 0.0), axis=-1, keepdims=True)
    fm2 = jnp.where(lane == i1, NEG, fm)
    f2max = jnp.max(fm2, axis=-1, keepdims=True)
    i2 = jnp.min(jnp.where(fm2 == f2max, lane, LANES), axis=-1, keepdims=True)
    p2 = jnp.sum(jnp.where(lane == i2, pe, 0.0), axis=-1, keepdims=True)
    g1 = p_top * (p1 / (p1 + p2))
    g2 = p_top * (p2 / (p1 + p2))
    e1 = (i1 - N_GROUPS).astype(F32)
    e2 = (i2 - N_GROUPS).astype(F32)
    r_ref[...] = jnp.where(lane == 0, e1, jnp.where(lane == 1, e2, jnp.where(lane == 2, g1,
                           jnp.where(lane == 3, g2, 0.0))))


def _router(x1, g, wr, br, tm):
    T = x1.shape[0]
    return pl.pallas_call(
        _router_kernel,
        grid=(T // tm,),
        in_specs=[pl.BlockSpec((tm, D_MODEL), lambda i: (i, 0)),
                  pl.BlockSpec((1, D_MODEL), lambda i: (0, 0)),
                  pl.BlockSpec((D_MODEL, LANES), lambda i: (0, 0)),
                  pl.BlockSpec((1, LANES), lambda i: (0, 0))],
        out_specs=[pl.BlockSpec((tm, D_MODEL), lambda i: (i, 0)),
                   pl.BlockSpec((tm, LANES), lambda i: (i, 0))],
        out_shape=[jax.ShapeDtypeStruct((T, D_MODEL), F32), jax.ShapeDtypeStruct((T, LANES), F32)],
        compiler_params=_params(("parallel",)),
        name="router",
    )(x1, g.reshape(1, D_MODEL), wr, br)


def _row_copy(src_hbm, row, dst, r, sem):
    return pltpu.make_async_copy(src_hbm.at[pl.ds(row, 1)], dst.at[pl.ds(r, 1)], sem)


def _moe_kernel(tok_ref, be_ref, nused_ref, x_hbm, w1_ref, w3_ref, w2_ref, o_ref, xbuf, sems):
    b = pl.program_id(0)
    nused = nused_ref[0]
    cur = lax.rem(b, 2)

    def gather(blk, buf, fn):
        for r in range(MOE_ROWS):
            fn(_row_copy(x_hbm, tok_ref[blk * MOE_ROWS + r], xbuf.at[buf], r, sems.at[buf]))

    def mlp(prefetch_next):
        gather(b, cur, lambda cp: cp.wait())
        if prefetch_next:
            gather(b + 1, 1 - cur, lambda cp: cp.start())
        x = xbuf[cur].astype(BF16)
        h1 = jnp.dot(x, w1_ref[...], preferred_element_type=F32)
        h3 = jnp.dot(x, w3_ref[...], preferred_element_type=F32)
        hid = (h1 * jax.nn.sigmoid(h1)) * h3
        o_ref[...] = jnp.dot(hid.astype(BF16), w2_ref[...], preferred_element_type=F32)

    @pl.when(b == 0)
    def _():
        gather(0, 0, lambda cp: cp.start())

    @pl.when(b + 1 < nused)
    def _():
        mlp(True)

    @pl.when(b + 1 == nused)
    def _():
        mlp(False)

    @pl.when(b >= nused)
    def _():
        o_ref[...] = jnp.zeros(o_ref.shape, o_ref.dtype)


def _moe_mlp(xn, slot_tok, w1, w3, w2, blk_exp, nused):
    ns = slot_tok.shape[0]
    nb = ns // MOE_ROWS
    once = pl.Buffered(1)
    return pl.pallas_call(
        _moe_kernel,
        grid_spec=pltpu.PrefetchScalarGridSpec(
            num_scalar_prefetch=3,
            grid=(nb,),
            in_specs=[
                pl.BlockSpec(memory_space=pl.ANY),
                pl.BlockSpec((None, D_MODEL, D_FF_EXP), lambda b, tok, be, nu: (be[b], 0, 0), pipeline_mode=once),
                pl.BlockSpec((None, D_MODEL, D_FF_EXP), lambda b, tok, be, nu: (be[b], 0, 0), pipeline_mode=once),
                pl.BlockSpec((None, D_FF_EXP, D_MODEL), lambda b, tok, be, nu: (be[b], 0, 0)),
            ],
            out_specs=pl.BlockSpec((MOE_ROWS, D_MODEL), lambda b, tok, be, nu: (b, 0)),
            scratch_shapes=[pltpu.VMEM((2, MOE_ROWS, D_MODEL), F32), pltpu.SemaphoreType.DMA((2,))],
        ),
        out_shape=jax.ShapeDtypeStruct((ns, D_MODEL), F32),
        compiler_params=_params(("arbitrary",)),
        name="moe_mlp",
    )(slot_tok, blk_exp, nused, xn, w1, w3, w2)


def _combine_kernel(pos_ref, x_ref, r_ref, o_hbm, y_ref, buf, sems, *, rows):
    i = pl.program_id(0)
    cur = lax.rem(i, 2)

    def gather(step, slot, fn):
        def body(r, carry):
            for k in range(2):
                fn(_row_copy(o_hbm, pos_ref[2 * (step * rows + r) + k], buf.at[slot, k], r, sems.at[slot]))
            return carry

        lax.fori_loop(0, rows, body, 0, unroll=4)

    @pl.when(i == 0)
    def _():
        gather(0, 0, lambda cp: cp.start())

    @pl.when(i + 1 < pl.num_programs(0))
    def _():
        gather(i + 1, 1 - cur, lambda cp: cp.start())

    gather(i, cur, lambda cp: cp.wait())
    r = r_ref[...]
    y_ref[...] = x_ref[...] + (buf[cur, 0] * r[:, 2:3] + buf[cur, 1] * r[:, 3:4])


def _combine(x1, route, o, pos, rows):
    T, D = x1.shape
    return pl.pallas_call(
        functools.partial(_combine_kernel, rows=rows),
        grid_spec=pltpu.PrefetchScalarGridSpec(
            num_scalar_prefetch=1,
            grid=(T // rows,),
            in_specs=[pl.BlockSpec((rows, D), lambda i, pos: (i, 0)),
                      pl.BlockSpec((rows, LANES), lambda i, pos: (i, 0)),
                      pl.BlockSpec(memory_space=pl.ANY)],
            out_specs=pl.BlockSpec((rows, D), lambda i, pos: (i, 0)),
            scratch_shapes=[pltpu.VMEM((2, 2, rows, D), F32), pltpu.SemaphoreType.DMA((2,))],
        ),
        out_shape=jax.ShapeDtypeStruct((T, D), F32),
        compiler_params=_params(("arbitrary",)),
        name="moe_combine",
    )(pos, x1, route, o)


def _dispatch_plan(route, T):
    i32 = jnp.int32
    tk = 2 * T
    nb = tk // MOE_ROWS + N_EXP
    experts = jnp.arange(N_EXP, dtype=i32)
    e_flat = route[:, 0:2].astype(i32).reshape(-1)
    order = jnp.argsort(e_flat).astype(i32)
    rank = jnp.argsort(order).astype(i32)
    counts = jnp.sum(e_flat[:, None] == experts[None, :], axis=0, dtype=i32)
    pcounts = (counts + MOE_ROWS - 1) // MOE_ROWS * MOE_ROWS
    pends = jnp.cumsum(pcounts)
    pstarts = pends - pcounts
    ustarts = jnp.cumsum(counts) - counts
    pos = pstarts[e_flat] + rank - ustarts[e_flat]
    slot = jnp.arange(nb * MOE_ROWS, dtype=i32)
    slot_exp = jnp.minimum(jnp.sum(pends[None, :] <= slot[:, None], axis=1, dtype=i32), N_EXP - 1)
    within = slot - pstarts[slot_exp]
    src = jnp.clip(ustarts[slot_exp] + within, 0, tk - 1)
    slot_tok = jnp.where(within < counts[slot_exp], order[src] // 2, 0).astype(i32)
    blk_exp = slot_exp[::MOE_ROWS]
    nused = (pends[-1:] // MOE_ROWS).astype(i32)
    return slot_tok, blk_exp, nused, pos.astype(i32)


def _tile(n, pref):
    return pref if n % pref == 0 else n


def _trunk(x, p):
    B, S, D = x.shape
    T = B * S
    xt = x.reshape(T, D)
    tm = _tile(T, PROJ_TM)
    tn = PROJ_TN
    tr = _tile(T, ROW_TM)

    xn = _rmsnorm(xt, p["norm_mix_g"], tr)
    q = _proj(xn, p["w_main"], W_A, BF16, tm, tn, col_off=0, vec=p["q_gain"], kind="headnorm", name="proj_q")
    kt = _proj(xn, p["w_main"], W_A, BF16, tm, tn, col_off=W_A, vec=p["k_gain"], kind="headnorm_t",
               name="proj_kt")
    vqkv = _proj(xn, p["w_main"], W_A + 2 * QK_B + W_B, BF16, tm, tn, col_off=2 * W_A, name="proj_vqkv")
    om = _proj(xn, p["w_main"], W_B, F32, tm, tn, col_off=3 * W_A + 2 * QK_B + W_B, name="proj_o")
    gates = _proj(xn, p["w_gate"], 2 * D_MODEL, F32, tm, tn, name="proj_gates")
    gif = _proj(xn, p["w_gif"], LANES, F32, tm, LANES, vec=p["b_gif"], kind="bias", name="proj_gif")

    ya = _diff_attention(q, kt, vqkv, p["slopes"], p["lam"], p["subln_scaled"], B, S,
                         _tile(S, ATTN_TQ), _tile(S, ATTN_TK))

    g4 = gif[:, :4 * N_HEADS_B].reshape(B, S, 4, N_HEADS_B)
    gcol = g4.transpose(0, 3, 1, 2)
    grow = g4.transpose(0, 3, 2, 1)
    hf, hb = _mlstm(vqkv, gcol, grow, B, S)
    yb = _mlstm_out(hf, hb, om, p["mlstm_norm_g"], tr)

    merged = _merge(ya, yb, p["w_branch_a"], p["w_branch_b"], gates, tm, MERGE_TN)
    x1 = _outproj(merged, p["w_out"], xt, tm, MERGE_TN)

    xn2, route = _router(x1, p["norm_ffn_g"], p["w_router"], p["b_router"], tr)
    slot_tok, blk_exp, nused, pos = _dispatch_plan(route, T)
    o = _moe_mlp(xn2, slot_tok, p["w1"], p["w3"], p["w2"], blk_exp, nused)
    y = _combine(x1, route, o, pos, _tile(T, 128))
    return y.reshape(B, S, D)


def _prepare(norm_mix_g, w_in, b_mgate, attn_q_norm_g, attn_k_norm_g, lambda_q1, lambda_k1, lambda_q2,
             lambda_k2, attn_subln_g, mlstm_norm_g, w_branch_a, w_branch_b, w_out, norm_ffn_g,
             w_router_group, b_router_group, w_router_expert, b_router_expert, w_exp_gate, w_exp_up,
             w_exp_down):
    l = 0
    n_main = 3 * W_A + 2 * QK_B + 2 * W_B
    n_if = 4 * N_HEADS_B
    w = w_in[l]
    q_gain = jnp.tile((attn_q_norm_g[l].astype(F32) * (HEAD_DIM_A ** -0.5 * LOG2E)).reshape(-1), N_HEADS_A)
    k_gain = jnp.tile(attn_k_norm_g[l].astype(F32).reshape(-1), N_HEADS_A)
    lam = (jnp.exp(jnp.sum(lambda_q1[l].astype(F32) * lambda_k1[l].astype(F32)))
           - jnp.exp(jnp.sum(lambda_q2[l].astype(F32) * lambda_k2[l].astype(F32))) + LAM_INIT)
    slopes = LOG2E * 2.0 ** (-8.0 * (jnp.arange(N_HEADS_A, dtype=F32) + 1.0) / N_HEADS_A)
    w_router = jnp.zeros((D_MODEL, LANES), F32)
    w_router = w_router.at[:, :N_GROUPS].set(w_router_group[l])
    w_router = w_router.at[:, N_GROUPS:N_GROUPS + N_EXP].set(w_router_expert[l])
    b_router = jnp.zeros((1, LANES), F32)
    b_router = b_router.at[0, :N_GROUPS].set(b_router_group[l].astype(F32))
    b_router = b_router.at[0, N_GROUPS:N_GROUPS + N_EXP].set(b_router_expert[l].astype(F32).reshape(-1))
    return {
        "norm_mix_g": norm_mix_g[l],
        "w_main": w[:, :n_main].astype(BF16),
        "w_gate": w[:, n_main + n_if:].astype(BF16),
        "w_gif": jnp.pad(w[:, n_main:n_main + n_if], ((0, 0), (0, LANES - n_if))).astype(BF16),
        "b_gif": jnp.pad(b_mgate[l].astype(F32), (0, LANES - n_if)),
        "q_gain": q_gain,
        "k_gain": k_gain,
        "lam": lam.reshape(1),
        "slopes": slopes,
        "subln_scaled": attn_subln_g[l].astype(F32) * (1.0 - LAM_INIT),
        "mlstm_norm_g": mlstm_norm_g[l],
        "w_branch_a": w_branch_a[l].astype(BF16),
        "w_branch_b": w_branch_b[l].astype(BF16),
        "w_out": w_out[l].astype(BF16),
        "norm_ffn_g": norm_ffn_g[l],
        "w_router": w_router,
        "b_router": b_router,
        "w1": w_exp_gate[l].astype(BF16),
        "w3": w_exp_up[l].astype(BF16),
        "w2": w_exp_down[l].astype(BF16),
    }


def kernel(x_prompt, x_sample, norm_mix_g, w_in, b_mgate, attn_q_norm_g, attn_k_norm_g, lambda_q1, lambda_k1, lambda_q2, lambda_k2, attn_subln_g, mlstm_norm_g, w_branch_a, w_branch_b, w_out, norm_ffn_g, w_router_group, b_router_group, w_router_expert, b_router_expert, w_exp_gate, w_exp_up, w_exp_down):
    p = _prepare(norm_mix_g, w_in, b_mgate, attn_q_norm_g, attn_k_norm_g, lambda_q1, lambda_k1, lambda_q2,
                 lambda_k2, attn_subln_g, mlstm_norm_g, w_branch_a, w_branch_b, w_out, norm_ffn_g,
                 w_router_group, b_router_group, w_router_expert, b_router_expert, w_exp_gate, w_exp_up,
                 w_exp_down)
    return (_trunk(x_prompt, p), _trunk(x_sample, p))
```
